```python
import math
import jax, jax.numpy as jnp
from jax import lax
import numpy as np

D_MODEL = 1024
BATCH = 2
SEQ = 8192
DEPTH = 2
DEC_BATCH = 32
DEC_SEQ = 4
PAST_LEN = 16384
PAGE_SIZE = 128

D_RNN = D_MODEL
RNN_BLOCKS = 16
RNN_BW = D_RNN // RNN_BLOCKS
RG_C = 8.0
CONV_W = 4
SSM_HEADS = 16
SSM_HEAD_DIM = 64
D_SSM = SSM_HEADS * SSM_HEAD_DIM
SSM_GROUPS = 2
SSM_STATE = 128
SSM_CONV_DIM = D_SSM + 2 * SSM_GROUPS * SSM_STATE
SSD_CHUNK = 128
D_IN0 = 2 * D_RNN + D_SSM + SSM_CONV_DIM + SSM_HEADS
ATT_HEADS = 8
ATT_HEAD_DIM = D_MODEL // ATT_HEADS // 2
ATT_V_DIM = 2 * ATT_HEAD_DIM
D_IN1 = 3 * ATT_HEADS * ATT_V_DIM
ATT_LAYER_IDX = 1
LAMBDA_INIT = 0.8 - 0.6 * math.exp(-0.3 * ATT_LAYER_IDX)
Q_BLOCK = 128
D_FF = 3 * D_MODEL
FFN_CONV_W = 3
EPS = 1e-6

kernel_name = 'hawk_ssd_diffattn_hybrid_step'


def rmsnorm(x, g):
    xf = x.astype(jnp.float32)
    y = xf * lax.rsqrt(jnp.mean(xf * xf, axis=-1, keepdims=True) + EPS)
    return (y * g.astype(jnp.float32)).astype(x.dtype)


def causal_dwconv(x, buf, w, b):
    width = w.shape[0]
    seq = x.shape[1]
    xp = jnp.concatenate([buf.astype(x.dtype), x], axis=1)
    y = xp[:, 0:seq] * w[0]
    for j in range(1, width):
        y = y + xp[:, j:j + seq] * w[j]
    return y + b, xp[:, xp.shape[1] - (width - 1):]


def rglru(xr, h0, pos, w_a, b_a, w_x, b_x, lam):
    f32 = jnp.float32
    bsz, seq, _ = xr.shape
    xf = xr.astype(f32)
    xb = xf.reshape(bsz, seq, RNN_BLOCKS, RNN_BW)
    r = jax.nn.sigmoid(jnp.einsum('blnj,njk->blnk', xb, w_a.astype(f32)).reshape(bsz, seq, D_RNN) + b_a.astype(f32))
    i = jax.nn.sigmoid(jnp.einsum('blnj,njk->blnk', xb, w_x.astype(f32)).reshape(bsz, seq, D_RNN) + b_x.astype(f32))
    log_a = -RG_C * r * jax.nn.softplus(-lam.astype(f32))
    reset = (pos == 0)[None, :, None]
    a = jnp.where(reset, 0.0, jnp.exp(log_a))
    mult = jnp.where(reset, 1.0, jnp.sqrt(jnp.maximum(1.0 - jnp.exp(2.0 * log_a), 0.0)))
    bv = xf * i * mult
    bv = bv.at[:, 0].add(a[:, 0] * h0.astype(f32))

    def combine(lhs, rhs):
        return (lhs[0] * rhs[0], rhs[0] * lhs[1] + rhs[1])

    _, h = lax.associative_scan(combine, (a, bv), axis=1)
    return h.astype(xr.dtype), h[:, -1].astype(xr.dtype)


def ssd_scan(x, dt, A, bm, cm, h0, chunk):
    b, l, h, p = x.shape
    g, n = bm.shape[2], bm.shape[3]
    k = h // g
    c = l // chunk
    xg = (x * dt[..., None]).reshape(b, c, chunk, g, k, p)
    a_cs = jnp.cumsum((dt * A).reshape(b, c, chunk, g, k), axis=2)
    bc = bm.reshape(b, c, chunk, g, n)
    cc = cm.reshape(b, c, chunk, g, n)
    causal = jnp.tril(jnp.ones((chunk, chunk), bool))[None, None, :, :, None, None]
    seg = a_cs[:, :, :, None] - a_cs[:, :, None, :]
    decay = jnp.exp(jnp.where(causal, seg, -jnp.inf))
    cb = jnp.einsum('bclgn,bcsgn->bclsg', cc, bc)
    y_diag = jnp.einsum('bclsg,bclsgk,bcsgkp->bclgkp', cb, decay, xg)
    decay_s = jnp.exp(a_cs[:, :, -1:] - a_cs)
    states = jnp.einsum('bclgn,bclgk,bclgkp->bcgkpn', bc, decay_s, xg)
    chunk_decay = jnp.exp(a_cs[:, :, -1])

    def step(hc, inp):
        s_c, d_c = inp
        return d_c[..., None, None] * hc + s_c, hc

    h_last, h_prev = lax.scan(step, h0.reshape(b, g, k, p, n),
                              (jnp.moveaxis(states, 1, 0), jnp.moveaxis(chunk_decay, 1, 0)))
    y_off = jnp.einsum('bclgn,cbgkpn,bclgk->bclgkp', cc, h_prev, jnp.exp(a_cs))
    y = (y_diag + y_off).reshape(b, l, h, p)
    return y, h_last.reshape(b, h, p, n)


def mixer_even(h, start_pos, chunk, rnn_buf, rnn_h0, ssm_buf, ssm_h0, w_in0, conv_rnn_w, conv_rnn_b,
               rg_w_a, rg_b_a, rg_w_x, rg_b_x, rg_lambda, conv_ssm_w, conv_ssm_b, dt_bias, a_log,
               d_skip, ssm_norm, w_out0):
    f32 = jnp.float32
    bsz, seq = h.shape[0], h.shape[1]
    proj = h @ w_in0
    x_rnn, y_rnn, z, xbc, dt_raw = jnp.split(
        proj, [D_RNN, 2 * D_RNN, 2 * D_RNN + D_SSM, 2 * D_RNN + D_SSM + SSM_CONV_DIM], axis=-1)
    xr, rnn_buf_new = causal_dwconv(x_rnn, rnn_buf, conv_rnn_w, conv_rnn_b)
    pos = start_pos + jnp.arange(seq)
    hr, rnn_h_new = rglru(xr, rnn_h0, pos, rg_w_a, rg_b_a, rg_w_x, rg_b_x, rg_lambda)
    rnn_out = hr * jax.nn.gelu(y_rnn)
    xbc, ssm_buf_new = causal_dwconv(xbc, ssm_buf, conv_ssm_w, conv_ssm_b)
    xbc = jax.nn.silu(xbc.astype(f32))
    xs, b_in, c_in = jnp.split(xbc, [D_SSM, D_SSM + SSM_GROUPS * SSM_STATE], axis=-1)
    xs = xs.reshape(bsz, seq, SSM_HEADS, SSM_HEAD_DIM)
    dt = jax.nn.softplus(dt_raw.astype(f32) + dt_bias.astype(f32))
    A = -jnp.exp(a_log.astype(f32))
    y, ssm_h_new = ssd_scan(xs, dt, A, b_in.reshape(bsz, seq, SSM_GROUPS, SSM_STATE),
                            c_in.reshape(bsz, seq, SSM_GROUPS, SSM_STATE), ssm_h0.astype(f32), chunk)
    y = y + d_skip.astype(f32)[:, None] * xs
    y = rmsnorm(y.reshape(bsz, seq, D_SSM) * jax.nn.silu(z.astype(f32)), ssm_norm).astype(h.dtype)
    out = jnp.concatenate([rnn_out, y], axis=-1) @ w_out0
    return out, rnn_buf_new, rnn_h_new, ssm_buf_new, ssm_h_new.astype(h.dtype)


def alibi_slopes():
    return jnp.asarray(2.0 ** (-8.0 * np.arange(1, ATT_HEADS + 1) / ATT_HEADS), dtype=jnp.float32)


def diff_probs(s, lam):
    p = jax.nn.softmax(s, axis=-1)
    return p[:, 0] - lam * p[:, 1]


def attn_prompt(q, k, v, lam, slopes):
    bsz, seq = q.shape[0], q.shape[1]
    nb = seq // Q_BLOCK
    qb = jnp.moveaxis(q.reshape(bsz, nb, Q_BLOCK, ATT_HEADS, 2, ATT_HEAD_DIM), 1, 0)
    kpos = jnp.arange(seq)
    scale = ATT_HEAD_DIM ** -0.5

    def block(args):
        q_blk, bi = args
        qpos = bi * Q_BLOCK + jnp.arange(Q_BLOCK)
        dist = (qpos[:, None] - kpos[None, :]).astype(jnp.float32)
        s = jnp.einsum('bqhmd,bkhmd->bmhqk', q_blk, k).astype(jnp.float32) * scale
        s = jnp.where(dist >= 0, s - slopes[:, None, None] * dist, -jnp.inf)
        attn = diff_probs(s, lam)
        return jnp.einsum('bhqk,bkhe->bqhe', attn.astype(v.dtype), v)

    out = lax.map(block, (qb, jnp.arange(nb)))
    return jnp.moveaxis(out, 0, 1).reshape(bsz, seq, ATT_HEADS, ATT_V_DIM)


def attn_sample(q, k_new, v_new, k_past, v_past, lam, slopes):
    t = q.shape[1]
    past = k_past.shape[1]
    scale = ATT_HEAD_DIM ** -0.5
    dist_past = ((past + jnp.arange(t))[:, None] - jnp.arange(past)[None, :]).astype(jnp.float32)
    dist_new = (jnp.arange(t)[:, None] - jnp.arange(t)[None, :]).astype(jnp.float32)
    s_past = jnp.einsum('bqhmd,bkhmd->bmhqk', q, k_past).astype(jnp.float32) * scale
    s_past = s_past - slopes[:, None, None] * dist_past
    s_new = jnp.einsum('bqhmd,bkhmd->bmhqk', q, k_new).astype(jnp.float32) * scale
    s_new = jnp.where(dist_new >= 0, s_new - slopes[:, None, None] * dist_new, -jnp.inf)
    attn = diff_probs(jnp.concatenate([s_past, s_new], axis=-1), lam).astype(v_new.dtype)
    return (jnp.einsum('bhqk,bkhe->bqhe', attn[..., :past], v_past)
            + jnp.einsum('bhqk,bkhe->bqhe', attn[..., past:], v_new))


def mixer_odd(h, k_past, v_past, w_in1, lambda_q1, lambda_k1, lambda_q2, lambda_k2, subln, w_out1):
    bsz, seq = h.shape[0], h.shape[1]
    q, k, v = jnp.split(h @ w_in1, 3, axis=-1)
    q = q.reshape(bsz, seq, ATT_HEADS, 2, ATT_HEAD_DIM)
    k = k.reshape(bsz, seq, ATT_HEADS, 2, ATT_HEAD_DIM)
    v = v.reshape(bsz, seq, ATT_HEADS, ATT_V_DIM)
    f32 = jnp.float32
    lam = (jnp.exp(jnp.sum(lambda_q1.astype(f32) * lambda_k1.astype(f32)))
           - jnp.exp(jnp.sum(lambda_q2.astype(f32) * lambda_k2.astype(f32))) + LAMBDA_INIT)
    slopes = alibi_slopes()
    if k_past is None:
        o = attn_prompt(q, k, v, lam, slopes)
    else:
        o = attn_sample(q, k, v, k_past, v_past, lam, slopes)
    o = rmsnorm(o, subln) * (1.0 - LAMBDA_INIT)
    out = o.reshape(bsz, seq, ATT_HEADS * ATT_V_DIM) @ w_out1
    return out, k.reshape(bsz, seq, ATT_HEADS, 2 * ATT_HEAD_DIM), v


def conv_ffn(h, buf, w_up, conv_w, conv_b, w_down):
    u, buf_new = causal_dwconv(h @ w_up, buf, conv_w, conv_b)
    g, val = jnp.split(u, 2, axis=-1)
    return (jax.nn.gelu(g) * val) @ w_down, buf_new


def trunk(x, start_pos, chunk, rnn_buf, rnn_h0, ssm_buf, ssm_h0, k_past, v_past, ffn_buf, w):
    (norm_mix, norm_ffn, norm_final, w_in0, conv_rnn_w, conv_rnn_b, rg_w_a, rg_b_a, rg_w_x, rg_b_x,
     rg_lambda, conv_ssm_w, conv_ssm_b, dt_bias, a_log, d_skip, ssm_norm, w_out0, w_in1, lambda_q1,
     lambda_k1, lambda_q2, lambda_k2, subln, w_out1, w_up, ffn_conv_w, ffn_conv_b, w_down) = w
    new_ffn = []
    for layer in range(DEPTH):
        hn = rmsnorm(x, norm_mix[layer])
        if layer % 2 == 0:
            mix, rnn_buf_n, rnn_h_n, ssm_buf_n, ssm_h_n = mixer_even(
                hn, start_pos, chunk, rnn_buf, rnn_h0, ssm_buf, ssm_h0, w_in0, conv_rnn_w, conv_rnn_b,
                rg_w_a, rg_b_a, rg_w_x, rg_b_x, rg_lambda, conv_ssm_w, conv_ssm_b, dt_bias, a_log,
                d_skip, ssm_norm, w_out0)
        else:
            mix, k_n, v_n = mixer_odd(hn, k_past, v_past, w_in1, lambda_q1, lambda_k1, lambda_q2,
                                      lambda_k2, subln, w_out1)
        x = x + mix
        f, fb = conv_ffn(rmsnorm(x, norm_ffn[layer]), ffn_buf[layer], w_up[layer], ffn_conv_w[layer],
                         ffn_conv_b[layer], w_down[layer])
        new_ffn.append(fb)
        x = x + f
    return (rmsnorm(x, norm_final), rnn_buf_n, rnn_h_n, ssm_buf_n, ssm_h_n, k_n, v_n, jnp.stack(new_ffn))


def setup_inputs(seed: int = 0) -> dict:
    key = jax.random.key(seed)
    ks = iter(jax.random.split(key, 48))
    f32 = jnp.float32

    def nrm(shape, scale):
        return jax.random.normal(next(ks), shape, f32) * scale

    def unif(shape, lo, hi):
        return jax.random.uniform(next(ks), shape, f32, lo, hi)

    n_pages = PAST_LEN // PAGE_SIZE
    n_used = DEC_BATCH * n_pages
    n_pool = n_used + n_used // 4
    page_table = jax.random.permutation(next(ks), n_pool)[:n_used].reshape(DEC_BATCH, n_pages).astype(jnp.int32)

    a0 = unif((D_RNN,), 0.9, 0.999)
    s0 = a0 ** (1.0 / RG_C)
    rg_lambda = jnp.log(s0) - jnp.log1p(-s0)
    dt0 = jnp.exp(unif((SSM_HEADS,), math.log(0.001), math.log(0.1)))
    dt_bias = dt0 + jnp.log(-jnp.expm1(-dt0))

    return {
        'x_prompt': nrm((BATCH, SEQ, D_MODEL), 1.0),
        'x_sample': nrm((DEC_BATCH, DEC_SEQ, D_MODEL), 1.0),
        'state_rnn_conv': nrm((DEC_BATCH, CONV_W - 1, D_RNN), 1.0),
        'state_rnn_h': nrm((DEC_BATCH, D_RNN), 0.5),
        'state_ssm_conv': nrm((DEC_BATCH, CONV_W - 1, SSM_CONV_DIM), 1.0),
        'state_ssm': nrm((DEC_BATCH, SSM_HEADS, SSM_HEAD_DIM, SSM_STATE), 0.1),
        'cache_k': nrm((n_pool, PAGE_SIZE, ATT_HEADS, 2 * ATT_HEAD_DIM), 1.0),
        'cache_v': nrm((n_pool, PAGE_SIZE, ATT_HEADS, ATT_V_DIM), 1.0),
        'state_ffn_conv': nrm((DEPTH, DEC_BATCH, FFN_CONV_W - 1, 2 * D_FF), 1.0),
        'page_table': page_table,
        'norm_mix': 1.0 + nrm((DEPTH, D_MODEL), 0.02),
        'norm_ffn': 1.0 + nrm((DEPTH, D_MODEL), 0.02),
        'norm_final': 1.0 + nrm((D_MODEL,), 0.02),
        'w_in0': nrm((D_MODEL, D_IN0), D_MODEL ** -0.5),
        'conv_rnn_w': nrm((CONV_W, D_RNN), CONV_W ** -0.5),
        'conv_rnn_b': nrm((D_RNN,), 0.02),
        'rg_w_a': nrm((RNN_BLOCKS, RNN_BW, RNN_BW), RNN_BW ** -0.5),
        'rg_b_a': nrm((D_RNN,), 0.02),
        'rg_w_x': nrm((RNN_BLOCKS, RNN_BW, RNN_BW), RNN_BW ** -0.5),
        'rg_b_x': nrm((D_RNN,), 0.02),
        'rg_lambda': rg_lambda,
        'conv_ssm_w': nrm((CONV_W, SSM_CONV_DIM), CONV_W ** -0.5),
        'conv_ssm_b': nrm((SSM_CONV_DIM,), 0.02),
        'dt_bias': dt_bias,
        'a_log': jnp.log(unif((SSM_HEADS,), 1.0, 16.0)),
        'd_skip': 1.0 + nrm((SSM_HEADS,), 0.02),
        'ssm_norm': 1.0 + nrm((D_SSM,), 0.02),
        'w_out0': nrm((D_RNN + D_SSM, D_MODEL), (D_RNN + D_SSM) ** -0.5),
        'w_in1': nrm((D_MODEL, D_IN1), D_MODEL ** -0.5),
        'lambda_q1': nrm((ATT_HEAD_DIM,), 0.1),
        'lambda_k1': nrm((ATT_HEAD_DIM,), 0.1),
        'lambda_q2': nrm((ATT_HEAD_DIM,), 0.1),
        'lambda_k2': nrm((ATT_HEAD_DIM,), 0.1),
        'subln': 1.0 + nrm((ATT_V_DIM,), 0.02),
        'w_out1': nrm((ATT_HEADS * ATT_V_DIM, D_MODEL), (ATT_HEADS * ATT_V_DIM) ** -0.5),
        'w_up': nrm((DEPTH, D_MODEL, 2 * D_FF), D_MODEL ** -0.5),
        'ffn_conv_w': nrm((DEPTH, FFN_CONV_W, 2 * D_FF), FFN_CONV_W ** -0.5),
        'ffn_conv_b': nrm((DEPTH, 2 * D_FF), 0.02),
        'w_down': nrm((DEPTH, D_FF, D_MODEL), D_FF ** -0.5),
    }


def reference(x_prompt, x_sample, state_rnn_conv, state_rnn_h, state_ssm_conv, state_ssm, cache_k, cache_v,
              state_ffn_conv, page_table, norm_mix, norm_ffn, norm_final, w_in0, conv_rnn_w, conv_rnn_b,
              rg_w_a, rg_b_a, rg_w_x, rg_b_x, rg_lambda, conv_ssm_w, conv_ssm_b, dt_bias, a_log, d_skip,
              ssm_norm, w_out0, w_in1, lambda_q1, lambda_k1, lambda_q2, lambda_k2, subln, w_out1, w_up,
              ffn_conv_w, ffn_conv_b, w_down):
    w = (norm_mix, norm_ffn, norm_final, w_in0, conv_rnn_w, conv_rnn_b, rg_w_a, rg_b_a, rg_w_x, rg_b_x,
         rg_lambda, conv_ssm_w, conv_ssm_b, dt_bias, a_log, d_skip, ssm_norm, w_out0, w_in1, lambda_q1,
         lambda_k1, lambda_q2, lambda_k2, subln, w_out1, w_up, ffn_conv_w, ffn_conv_b, w_down)
    dt_ = x_prompt.dtype
    bsz = x_prompt.shape[0]
    (y_prompt, p_rnn_conv, p_rnn_h, p_ssm_conv, p_ssm, p_k, p_v, p_ffn_conv) = trunk(
        x_prompt, 0, SSD_CHUNK,
        jnp.zeros((bsz, CONV_W - 1, D_RNN), dt_), jnp.zeros((bsz, D_RNN), dt_),
        jnp.zeros((bsz, CONV_W - 1, SSM_CONV_DIM), dt_),
        jnp.zeros((bsz, SSM_HEADS, SSM_HEAD_DIM, SSM_STATE), dt_),
        None, None, jnp.zeros((DEPTH, bsz, FFN_CONV_W - 1, 2 * D_FF), dt_), w)
    dbsz, n_pages = page_table.shape
    past = n_pages * PAGE_SIZE
    k_past = cache_k[page_table].reshape(dbsz, past, ATT_HEADS, 2, ATT_HEAD_DIM)
    v_past = cache_v[page_table].reshape(dbsz, past, ATT_HEADS, ATT_V_DIM)
    (y_sample, s_rnn_conv, s_rnn_h, s_ssm_conv, s_ssm, s_k, s_v, s_ffn_conv) = trunk(
        x_sample, past, x_sample.shape[1], state_rnn_conv, state_rnn_h, state_ssm_conv, state_ssm,
        k_past, v_past, state_ffn_conv, w)
    return (y_prompt, y_sample, p_rnn_conv, p_rnn_h, p_ssm_conv, p_ssm, p_k, p_v, p_ffn_conv,
            s_rnn_conv, s_rnn_h, s_ssm_conv, s_ssm, s_k, s_v, s_ffn_conv)
```

```python
import functools
import math

import jax
import jax.numpy as jnp
import numpy as np
from jax import lax
from jax.experimental import pallas as pl
from jax.experimental.pallas import tpu as pltpu

F32 = jnp.float32
BF16 = jnp.bfloat16

D_MODEL = 1024
D_RNN = 1024
RNN_BW = 64
RG_C = 8.0
CONV_W = 4
SSM_HEADS = 16
SSM_HEAD_DIM = 64
D_SSM = SSM_HEADS * SSM_HEAD_DIM
SSM_GROUPS = 2
SSM_STATE = 128
SSM_CONV_DIM = D_SSM + 2 * SSM_GROUPS * SSM_STATE
SSD_CHUNK = 128
D_IN0 = 2 * D_RNN + D_SSM + SSM_CONV_DIM + SSM_HEADS
D_IN0_PAD = 2 * D_RNN + D_SSM + SSM_CONV_DIM + 128
ATT_HEADS = 8
ATT_HEAD_DIM = 64
ATT_V_DIM = 128
LAMBDA_INIT = 0.8 - 0.6 * math.exp(-0.3 * 1)
D_FF = 3 * D_MODEL
FFN_CONV_W = 3
EPS = 1e-6
PAGE_SIZE = 128

SUBLANES = 8
LANES = 128
HALO = SUBLANES
GROUP = 8
TOK0 = 4
VMEM_LIMIT = 56 * 1024 * 1024


def _cparams(sem):
    return pltpu.CompilerParams(dimension_semantics=sem, vmem_limit_bytes=VMEM_LIMIT)


def _resident(shape):
    nd = len(shape)
    return pl.BlockSpec(shape, lambda *_: (0,) * nd, pipeline_mode=pl.Buffered(1))


def _rms(x, g):
    var = jnp.mean(x * x, axis=-1, keepdims=True)
    return x * lax.rsqrt(var + EPS) * g


def _softplus(x):
    return jnp.maximum(x, 0.0) + jnp.log1p(jnp.exp(-jnp.abs(x)))


def _dot(a, b):
    return jnp.dot(a, b, preferred_element_type=F32)


def _dot_nt(a, b):
    return lax.dot_general(a, b, (((1,), (1,)), ((), ())), preferred_element_type=F32)


def _proj0_kernel(x_ref, g_ref, w_ref, o_ref, *, tn):
    xn = _rms(x_ref[...], g_ref[...]).astype(BF16)
    n = w_ref.shape[1]
    for c in range(0, n, tn):
        w = min(tn, n - c)
        o_ref[:, c:c + w] = _dot(xn, w_ref[:, c:c + w])


def proj0(x, g, w, tm):
    m, d = x.shape
    n = w.shape[1]
    return pl.pallas_call(
        functools.partial(_proj0_kernel, tn=512),
        grid=(m // tm,),
        in_specs=[pl.BlockSpec((tm, d), lambda i: (i, 0)), _resident((1, d)), _resident((d, n))],
        out_specs=pl.BlockSpec((tm, n), lambda i: (i, 0)),
        out_shape=jax.ShapeDtypeStruct((m, n), F32),
        compiler_params=_cparams(("parallel",)),
        name="proj0",
    )(x, g, w)


def _proj1_kernel(x_ref, g_ref, w_ref, k_ref, v_ref, qkv_ref, *, tn):
    xn = _rms(x_ref[...], g_ref[...]).astype(BF16)
    d = k_ref.shape[1]
    for c in range(0, 3 * d, tn):
        r = _dot(xn, w_ref[:, c:c + tn])
        if c < d:
            qkv_ref[:, c:c + tn] = (r * (ATT_HEAD_DIM ** -0.5)).astype(BF16)
        else:
            qkv_ref[:, c:c + tn] = r.astype(BF16)
            if c < 2 * d:
                k_ref[:, c - d:c - d + tn] = r
            else:
                v_ref[:, c - 2 * d:c - 2 * d + tn] = r


def proj1(x, g, w, tm):
    m, d = x.shape
    n = w.shape[1]
    return pl.pallas_call(
        functools.partial(_proj1_kernel, tn=512),
        grid=(m // tm,),
        in_specs=[pl.BlockSpec((tm, d), lambda i: (i, 0)), _resident((1, d)), _resident((d, n))],
        out_specs=[pl.BlockSpec((tm, d), lambda i: (i, 0)), pl.BlockSpec((tm, d), lambda i: (i, 0)),
                   pl.BlockSpec((tm, n), lambda i: (i, 0))],
        out_shape=[jax.ShapeDtypeStruct((m, d), F32), jax.ShapeDtypeStruct((m, d), F32),
                   jax.ShapeDtypeStruct((m, n), BF16)],
        compiler_params=_cparams(("parallel",)),
        name="proj1",
    )(x, g, w)


def _out_proj_kernel(*refs, n_in, tn):
    a_refs = refs[:n_in]
    w_refs = refs[n_in:2 * n_in]
    res_ref = refs[2 * n_in]
    o_ref = refs[2 * n_in + 1]
    n = o_ref.shape[1]
    for c in range(0, n, tn):
        acc = res_ref[:, c:c + tn]
        for a_ref, w_ref in zip(a_refs, w_refs):
            acc = acc + _dot(a_ref[...], w_ref[:, c:c + tn])
        o_ref[:, c:c + tn] = acc


def out_proj(a_list, w_list, res, tm):
    m, n = res.shape
    n_in = len(a_list)
    in_specs = [pl.BlockSpec((tm, a.shape[1]), lambda i: (i, 0)) for a in a_list]
    in_specs += [_resident(w.shape) for w in w_list]
    in_specs += [pl.BlockSpec((tm, n), lambda i: (i, 0))]
    return pl.pallas_call(
        functools.partial(_out_proj_kernel, n_in=n_in, tn=512),
        grid=(m // tm,),
        in_specs=in_specs,
        out_specs=pl.BlockSpec((tm, n), lambda i: (i, 0)),
        out_shape=jax.ShapeDtypeStruct((m, n), F32),
        compiler_params=_cparams(("parallel",)),
        name="out_proj",
    )(*a_list, *w_list, res)


def _conv_from_ext(ext_ref, w_ref, b_ref, width, tm, c0, c1):
    y = b_ref[:, c0:c1]
    for j in range(width):
        off = HALO - (width - 1) + j
        y = y + ext_ref[off:off + tm, c0:c1] * w_ref[j:j + 1, c0:c1]
    return y


def _rglru_kernel(*refs, tm, sample):
    if sample:
        (x_ref, y_ref, cw_ref, cb_ref, wa_ref, ba_ref, wx_ref, bx_ref, lam_ref, hinj_ref,
         out_ref, h_out_ref, ext_ref, a_ref, b_ref, h_ref, carry_ref) = refs
    else:
        (x_ref, y_ref, halo_ref, cw_ref, cb_ref, wa_ref, ba_ref, wx_ref, bx_ref, lam_ref,
         out_ref, h_out_ref, ext_ref, a_ref, b_ref, h_ref, carry_ref) = refs
    i = pl.program_id(1)
    d = x_ref.shape[1]

    @pl.when(i == 0)
    def _():
        carry_ref[...] = jnp.zeros_like(carry_ref)

    if sample:
        ext_ref[0:HALO, :] = jnp.zeros((HALO, d), F32)
    else:
        ext_ref[0:HALO, :] = jnp.where(i == 0, 0.0, halo_ref[...])
    ext_ref[HALO:HALO + tm, :] = x_ref[...]

    row = lax.broadcasted_iota(jnp.int32, (tm, 1), 0)
    sp = _softplus(-lam_ref[...])
    cbw = 4 * RNN_BW
    for c in range(d // cbw):
        c0, c1 = c * cbw, (c + 1) * cbw
        xr = _conv_from_ext(ext_ref, cw_ref, cb_ref, CONV_W, tm, c0, c1)
        xb = xr.astype(BF16)
        r = jax.nn.sigmoid(_dot(xb, wa_ref[c]) + ba_ref[:, c0:c1])
        ig = jax.nn.sigmoid(_dot(xb, wx_ref[c]) + bx_ref[:, c0:c1])
        log_a = -RG_C * r * sp[:, c0:c1]
        a = jnp.exp(log_a)
        mult = jnp.sqrt(jnp.maximum(1.0 - a * a, 0.0))
        if sample:
            inj = (row % GROUP) == (TOK0 - 1)
            a = jnp.where(inj, 0.0, a)
            bv = jnp.where(inj, hinj_ref[:, c0:c1], xr * ig * mult)
        else:
            first = jnp.logical_and(i == 0, row == 0)
            a = jnp.where(first, 0.0, a)
            mult = jnp.where(first, 1.0, mult)
            bv = xr * ig * mult
        for k in range(cbw // LANES):
            kt = c0 // LANES + k
            a_ref[kt] = a[:, k * LANES:(k + 1) * LANES]
            b_ref[kt] = bv[:, k * LANES:(k + 1) * LANES]

    g_n = tm // SUBLANES
    tail = h_out_ref.shape[0]
    for kt in range(d // LANES):
        l0, l1 = kt * LANES, (kt + 1) * LANES
        hl = jnp.zeros((SUBLANES, LANES), F32)
        pr = jnp.ones((SUBLANES, LANES), F32)
        for g in range(g_n):
            idx = pl.ds(g, SUBLANES, stride=g_n)
            av = a_ref[kt, idx, :]
            hl = av * hl + b_ref[kt, idx, :]
            pr = av * pr
            h_ref[kt, idx, :] = hl
            a_ref[kt, idx, :] = pr
        cin = carry_ref[:, l0:l1]
        rows = []
        for j in range(SUBLANES):
            rows.append(cin)
            cin = hl[j:j + 1, :] + pr[j:j + 1, :] * cin
        carry_ref[:, l0:l1] = cin
        seg_in = jnp.concatenate(rows, axis=0)
        for g in range(g_n):
            idx = pl.ds(g, SUBLANES, stride=g_n)
            h_ref[kt, idx, :] = h_ref[kt, idx, :] + a_ref[kt, idx, :] * seg_in
        h = h_ref[kt]
        out_ref[:, l0:l1] = (h * jax.nn.gelu(y_ref[:, l0:l1])).astype(out_ref.dtype)
        h_out_ref[:, l0:l1] = h_ref[kt, tm - tail:tm, :]


def rglru(proj, nseq, seq, tm, cw, cb, wa, ba, wx, bx, lam, hinj=None):
    sample = hinj is not None
    nt = seq // tm
    d = D_RNN
    xmap = lambda b, i: (b * nt + i, 0)
    in_specs = [pl.BlockSpec((tm, d), xmap), pl.BlockSpec((tm, d), lambda b, i: (b * nt + i, 1))]
    args = [proj, proj]
    if not sample:
        hb = tm // HALO
        in_specs.append(pl.BlockSpec((HALO, d), lambda b, i: (jnp.maximum((b * nt + i) * hb - 1, 0), 0)))
        args.append(proj)
    in_specs += [_resident(cw.shape), _resident(cb.shape), _resident(wa.shape), _resident(ba.shape),
                 _resident(wx.shape), _resident(bx.shape), _resident(lam.shape)]
    args += [cw, cb, wa, ba, wx, bx, lam]
    if sample:
        in_specs.append(pl.BlockSpec((tm, d), xmap))
        args.append(hinj)
        h_spec = pl.BlockSpec((tm, d), xmap)
        h_shape = jax.ShapeDtypeStruct((nseq * seq, d), F32)
    else:
        h_spec = pl.BlockSpec((None, SUBLANES, d), lambda b, i: (b, 0, 0))
        h_shape = jax.ShapeDtypeStruct((nseq, SUBLANES, d), F32)
    return pl.pallas_call(
        functools.partial(_rglru_kernel, tm=tm, sample=sample),
        grid=(nseq, nt),
        in_specs=in_specs,
        out_specs=[pl.BlockSpec((tm, d), xmap), h_spec],
        out_shape=[jax.ShapeDtypeStruct((nseq * seq, d), BF16), h_shape],
        scratch_shapes=[pltpu.VMEM((tm + HALO, d), F32)] + [pltpu.VMEM((d // LANES, tm, LANES), F32)] * 3
        + [pltpu.VMEM((1, d), F32)],
        compiler_params=_cparams(("parallel", "arbitrary")),
        name="rglru",
    )(*args)


def _pair(v, hp, lane_lo):
    return jnp.where(lane_lo, v[:, 2 * hp:2 * hp + 1], v[:, 2 * hp + 1:2 * hp + 2])


def _ssd_kernel(*refs, lc, n_valid, has_init):
    if has_init:
        (xbc_ref, halo_ref, z_ref, dt_ref, cw_ref, cb_ref, dtb_ref, alog_ref, dsk_ref, nrm_ref, s0_ref,
         y_ref, sout_ref, ext_ref, st_ref, yacc_ref) = refs
    else:
        (xbc_ref, halo_ref, z_ref, dt_ref, cw_ref, cb_ref, dtb_ref, alog_ref, dsk_ref, nrm_ref,
         y_ref, sout_ref, ext_ref, st_ref, yacc_ref) = refs
    i = pl.program_id(1)
    n = SSM_STATE
    hd2 = 2 * SSM_HEAD_DIM
    gw = D_SSM // SSM_GROUPS

    @pl.when(i == 0)
    def _():
        if has_init:
            st_ref[...] = s0_ref[...]
        else:
            st_ref[...] = jnp.zeros_like(st_ref)

    if has_init:
        ext_ref[0:HALO, :] = halo_ref[...]
    else:
        ext_ref[0:HALO, :] = jnp.where(i == 0, 0.0, halo_ref[...])
    ext_ref[HALO:HALO + lc, :] = xbc_ref[...]
    xbc = _conv_from_ext(ext_ref, cw_ref, cb_ref, CONV_W, lc, 0, SSM_CONV_DIM)
    xbc = xbc * jax.nn.sigmoid(xbc)
    xs = xbc[:, 0:D_SSM]

    row = lax.broadcasted_iota(jnp.int32, (lc, 1), 0)
    dt = _softplus(dt_ref[...] + dtb_ref[...])
    if n_valid < lc:
        dt = jnp.where(row < n_valid, dt, 0.0)
    acs = dt * (-jnp.exp(alog_ref[...]))
    s = 1
    while s < lc:
        acs = acs + jnp.where(row >= s, pltpu.roll(acs, s, axis=0), 0.0)
        s *= 2
    if lc == LANES:
        acs_t = acs.T
    else:
        acs_t = jnp.concatenate([acs, jnp.zeros((LANES - lc, LANES), F32)], axis=0).T[:, 0:lc]
    a_last = acs[lc - 1:lc, :]
    ea = jnp.exp(acs)
    dsv = jnp.exp(a_last - acs)
    cdec = jnp.exp(a_last)

    lane_lo = lax.broadcasted_iota(jnp.int32, (1, hd2), 1) < SSM_HEAD_DIM
    causal = (lax.broadcasted_iota(jnp.int32, (lc, lc), 0) >= lax.broadcasted_iota(jnp.int32, (lc, lc), 1))
    for g in range(SSM_GROUPS):
        bm = xbc[:, D_SSM + g * n:D_SSM + (g + 1) * n]
        cm = xbc[:, D_SSM + SSM_GROUPS * n + g * n:D_SSM + SSM_GROUPS * n + (g + 1) * n]
        bm_b = bm.astype(BF16)
        cm_b = cm.astype(BF16)
        cbm = _dot_nt(cm_b, bm_b)
        st_old = st_ref[:, g * gw:(g + 1) * gw]
        yoff = _dot(cm_b, st_old.astype(BF16))
        if lc == LANES:
            bm_t = bm.T.astype(BF16)
        else:
            bm_t = jnp.concatenate([bm, jnp.zeros((LANES - lc, n), F32)], axis=0).T[:, 0:lc].astype(BF16)
        w_parts = []
        cd_parts = []
        for pp in range(gw // hd2):
            hp = g * (gw // hd2) + pp
            h0, h1 = 2 * hp, 2 * hp + 1
            c0 = hp * hd2
            xg = xs[:, c0:c0 + hd2] * _pair(dt, hp, lane_lo)
            xg_b = xg.astype(BF16)
            yd = []
            for h in (h0, h1):
                seg = acs[:, h:h + 1] - acs_t[h:h + 1, :]
                dec = jnp.exp(jnp.where(causal, seg, -jnp.inf))
                yd.append(_dot((cbm * dec).astype(BF16), xg_b))
            ydiag = jnp.where(lane_lo, yd[0], yd[1])
            yo = yoff[:, pp * hd2:(pp + 1) * hd2] * _pair(ea, hp, lane_lo)
            yacc_ref[:, c0:c0 + hd2] = ydiag + yo + dsk_ref[:, c0:c0 + hd2] * xs[:, c0:c0 + hd2]
            w_parts.append((xg * _pair(dsv, hp, lane_lo)).astype(BF16))
            cd_parts.append(_pair(cdec, hp, lane_lo))
        w_all = jnp.concatenate(w_parts, axis=1)
        cd_all = jnp.concatenate(cd_parts, axis=1)
        st_ref[:, g * gw:(g + 1) * gw] = cd_all * st_old + _dot(bm_t, w_all)

    z = z_ref[...]
    y = yacc_ref[...] * (z * jax.nn.sigmoid(z))
    y_ref[...] = _rms(y, nrm_ref[...]).astype(y_ref.dtype)
    sout_ref[...] = st_ref[...]


def ssd(xbc_arr, xbc_col, halo_arr, halo_map, z_arr, z_col, dt_arr, dt_col, nseq, nchunk, lc, n_valid,
        cw, cb, dtb, alog, dsk, nrm, s0=None):
    has_init = s0 is not None
    rmap = lambda col: (lambda b, i: (b * nchunk + i, col))
    in_specs = [pl.BlockSpec((lc, SSM_CONV_DIM), rmap(xbc_col)),
                halo_map,
                pl.BlockSpec((lc, D_SSM), rmap(z_col)),
                pl.BlockSpec((lc, LANES), rmap(dt_col)),
                _resident(cw.shape), _resident(cb.shape), _resident(dtb.shape), _resident(alog.shape),
                _resident(dsk.shape), _resident(nrm.shape)]
    args = [xbc_arr, halo_arr, z_arr, dt_arr, cw, cb, dtb, alog, dsk, nrm]
    if has_init:
        in_specs.append(pl.BlockSpec((None, SSM_STATE, D_SSM), lambda b, i: (b, 0, 0)))
        args.append(s0)
    return pl.pallas_call(
        functools.partial(_ssd_kernel, lc=lc, n_valid=n_valid, has_init=has_init),
        grid=(nseq, nchunk),
        in_specs=in_specs,
        out_specs=[pl.BlockSpec((lc, D_SSM), lambda b, i: (b * nchunk + i, 0)),
                   pl.BlockSpec((None, SSM_STATE, D_SSM), lambda b, i: (b, 0, 0))],
        out_shape=[jax.ShapeDtypeStruct((nseq * nchunk * lc, D_SSM), BF16),
                   jax.ShapeDtypeStruct((nseq, SSM_STATE, D_SSM), F32)],
        scratch_shapes=[pltpu.VMEM((lc + HALO, SSM_CONV_DIM), F32), pltpu.VMEM((SSM_STATE, D_SSM), F32),
                        pltpu.VMEM((lc, D_SSM), F32)],
        compiler_params=_cparams(("parallel", "arbitrary")),
        name="ssd",
    )(*args)


def _ffn_kernel(*refs, tm, fc, sample, final_norm):
    if sample:
        (x_ref, g_ref, wup_ref, cw_ref, cb_ref, wdn_ref, gf_ref, uinj_ref,
         o_ref, u_out_ref, ext_ref) = refs
    else:
        (x_ref, halo_ref, g_ref, wup_ref, cw_ref, cb_ref, wdn_ref, gf_ref,
         o_ref, u_out_ref, ext_ref) = refs
    i = pl.program_id(1)
    d_ff = wdn_ref.shape[0]
    x = x_ref[...]
    hn = _rms(x, g_ref[...]).astype(BF16)
    if sample:
        row = lax.broadcasted_iota(jnp.int32, (tm, 1), 0) % GROUP
        hist = jnp.logical_and(row >= TOK0 - (FFN_CONV_W - 1), row < TOK0)
    else:
        halo_n = _rms(halo_ref[...], g_ref[...]).astype(BF16)
    tail = u_out_ref.shape[0]
    acc = x
    for c in range(0, d_ff, fc):
        conv = []
        for base in (c, d_ff + c):
            u = _dot(hn, wup_ref[:, base:base + fc])
            if sample:
                u = jnp.where(hist, uinj_ref[:, base:base + fc], u)
                ext_ref[0:HALO, :] = jnp.zeros((HALO, fc), F32)
            else:
                uh = _dot(halo_n, wup_ref[:, base:base + fc])
                ext_ref[0:HALO, :] = jnp.where(i == 0, 0.0, uh)
            ext_ref[HALO:HALO + tm, :] = u
            u_out_ref[:, base:base + fc] = ext_ref[HALO + tm - tail:HALO + tm, :]
            y = cb_ref[:, base:base + fc]
            for j in range(FFN_CONV_W):
                off = HALO - (FFN_CONV_W - 1) + j
                y = y + ext_ref[off:off + tm, :] * cw_ref[j:j + 1, base:base + fc]
            conv.append(y)
        act = (jax.nn.gelu(conv[0]) * conv[1]).astype(BF16)
        acc = acc + _dot(act, wdn_ref[c:c + fc, :])
    if final_norm:
        acc = _rms(acc, gf_ref[...])
    o_ref[...] = acc


def ffn(x, nseq, seq, tm, g, wup, cw, cb, wdn, gf, final_norm, uinj=None):
    sample = uinj is not None
    nt = seq // tm
    d = x.shape[1]
    d_up = wup.shape[1]
    xmap = lambda b, i: (b * nt + i, 0)
    in_specs = [pl.BlockSpec((tm, d), xmap)]
    args = [x]
    if not sample:
        hb = tm // HALO
        in_specs.append(pl.BlockSpec((HALO, d), lambda b, i: (jnp.maximum((b * nt + i) * hb - 1, 0), 0)))
        args.append(x)
    in_specs += [_resident(g.shape), _resident(wup.shape), _resident(cw.shape), _resident(cb.shape),
                 _resident(wdn.shape), _resident(gf.shape)]
    args += [g, wup, cw, cb, wdn, gf]
    if sample:
        in_specs.append(pl.BlockSpec((tm, d_up), xmap))
        args.append(uinj)
        u_spec = pl.BlockSpec((tm, d_up), xmap)
        u_shape = jax.ShapeDtypeStruct((nseq * seq, d_up), F32)
    else:
        u_spec = pl.BlockSpec((None, SUBLANES, d_up), lambda b, i: (b, 0, 0))
        u_shape = jax.ShapeDtypeStruct((nseq, SUBLANES, d_up), F32)
    fc = 512
    return pl.pallas_call(
        functools.partial(_ffn_kernel, tm=tm, fc=fc, sample=sample, final_norm=final_norm),
        grid=(nseq, nt),
        in_specs=in_specs,
        out_specs=[pl.BlockSpec((tm, d), xmap), u_spec],
        out_shape=[jax.ShapeDtypeStruct((nseq * seq, d), F32), u_shape],
        scratch_shapes=[pltpu.VMEM((tm + HALO, fc), F32)],
        compiler_params=_cparams(("parallel", "arbitrary")),
        name="ffn",
    )(*args)


def _lambda_full(lam_ref):
    lq1 = lam_ref[0:1, :]
    lk1 = lam_ref[1:2, :]
    lq2 = lam_ref[2:3, :]
    lk2 = lam_ref[3:4, :]
    return (jnp.exp(jnp.sum(lq1 * lk1, axis=-1, keepdims=True))
            - jnp.exp(jnp.sum(lq2 * lk2, axis=-1, keepdims=True)) + LAMBDA_INIT)


def _alibi_slopes():
    return np.asarray(2.0 ** (-8.0 * np.arange(1, ATT_HEADS + 1) / ATT_HEADS), dtype=np.float32)


def _attn_prompt_kernel(q_ref, k_ref, v_ref, slope_ref, lam_ref, sub_ref, o_ref, m_ref, l_ref, acc_ref, *, t):
    h = pl.program_id(1)
    i = pl.program_id(2)
    dh = ATT_HEAD_DIM
    slope = slope_ref[pl.ds(h, 1), :][:, 0:1]
    q = q_ref[...]
    lane = lax.broadcasted_iota(jnp.int32, (1, 2 * dh), 1)
    zero = jnp.zeros_like(q)
    qq = jnp.concatenate([jnp.where(lane < dh, q, zero), jnp.where(lane >= dh, q, zero)], axis=0)
    rr = lax.broadcasted_iota(jnp.int32, (t, t), 0)
    cc = lax.broadcasted_iota(jnp.int32, (t, t), 1)
    bias0 = (cc - rr).astype(F32) * slope
    bias2 = jnp.concatenate([bias0, bias0], axis=0)
    causal2 = jnp.concatenate([cc <= rr, cc <= rr], axis=0)

    m_ref[...] = jnp.full_like(m_ref, -jnp.inf)
    l_ref[...] = jnp.zeros_like(l_ref)
    acc_ref[...] = jnp.zeros_like(acc_ref)

    def block(j, diag):
        kj = k_ref[pl.ds(pl.multiple_of(j * t, t), t), :]
        vj = v_ref[pl.ds(pl.multiple_of(j * t, t), t), :]
        s = _dot_nt(qq, kj) + bias2
        if diag:
            s = jnp.where(causal2, s, -jnp.inf)
        off = slope * jnp.full((1, 1), (j - i) * t, jnp.int32).astype(F32)
        m_old = m_ref[...]
        m_new = jnp.maximum(m_old, jnp.max(s, axis=-1, keepdims=True) + off)
        p = jnp.exp(s - (m_new - off))
        alpha = jnp.exp(m_old - m_new)
        l_ref[...] = alpha * l_ref[...] + jnp.sum(p, axis=-1, keepdims=True)
        acc_ref[...] = alpha * acc_ref[...] + _dot(p.astype(BF16), vj)
        m_ref[...] = m_new

    def body(j, carry):
        block(j, False)
        return carry

    lax.fori_loop(0, i, body, 0)
    block(i, True)

    lam = _lambda_full(lam_ref)
    o = acc_ref[...] / l_ref[...]
    o = o[0:t, :] - lam * o[t:2 * t, :]
    o_ref[...] = (_rms(o, sub_ref[...]) * (1.0 - LAMBDA_INIT)).astype(o_ref.dtype)


def attn_prompt(qkv, nseq, seq, t, lam4, subln):
    nq = seq // t
    hw = 2 * ATT_HEAD_DIM
    qkv3 = qkv.reshape(nseq, seq, qkv.shape[1])
    slopes = jnp.asarray(np.tile(_alibi_slopes()[:, None], (1, LANES)))
    return pl.pallas_call(
        functools.partial(_attn_prompt_kernel, t=t),
        grid=(nseq, ATT_HEADS, nq),
        in_specs=[pl.BlockSpec((None, t, hw), lambda b, h, i: (b, i, h)),
                  pl.BlockSpec((None, seq, hw), lambda b, h, i: (b, 0, ATT_HEADS + h)),
                  pl.BlockSpec((None, seq, hw), lambda b, h, i: (b, 0, 2 * ATT_HEADS + h)),
                  _resident(slopes.shape), _resident(lam4.shape), _resident(subln.shape)],
        out_specs=pl.BlockSpec((None, t, hw), lambda b, h, i: (b, i, h)),
        out_shape=jax.ShapeDtypeStruct((nseq, seq, ATT_HEADS * ATT_V_DIM), BF16),
        scratch_shapes=[pltpu.VMEM((2 * t, 1), F32), pltpu.VMEM((2 * t, 1), F32), pltpu.VMEM((2 * t, hw), F32)],
        compiler_params=_cparams(("parallel", "parallel", "arbitrary")),
        name="attn_prompt",
    )(qkv3, qkv3, qkv3, slopes, lam4, subln).reshape(nseq * seq, ATT_HEADS * ATT_V_DIM)


def _attn_sample_kernel(pt_ref, *refs, npp, past):
    k_refs = refs[:npp]
    v_refs = refs[npp:2 * npp]
    qbd_ref, knew_ref, vnew_ref, slope_ref, lam_ref, sub_ref, o_ref, m_ref, l_ref, acc_ref = refs[2 * npp:]
    j = pl.program_id(1)
    nj = pl.num_programs(1)
    nr = qbd_ref.shape[0]
    rowi = lax.broadcasted_iota(jnp.int32, (nr, 1), 0)
    slope = slope_ref[:, 0:1]
    tq = (rowi % TOK0).astype(F32)

    @pl.when(j == 0)
    def _():
        m_ref[...] = jnp.full_like(m_ref, -jnp.inf)
        l_ref[...] = jnp.zeros_like(l_ref)
        acc_ref[...] = jnp.zeros_like(acc_ref)

    qbd = qbd_ref[...]

    def update(s, v_list):
        m_old = m_ref[...]
        m_new = jnp.maximum(m_old, jnp.max(s, axis=-1, keepdims=True))
        p = jnp.exp(s - m_new)
        alpha = jnp.exp(m_old - m_new)
        l_ref[...] = alpha * l_ref[...] + jnp.sum(p, axis=-1, keepdims=True)
        pv = None
        w = p.shape[1] // len(v_list)
        for r, vb in enumerate(v_list):
            t = _dot(p[:, r * w:(r + 1) * w].astype(BF16), vb)
            pv = t if pv is None else pv + t
        acc_ref[...] = alpha * acc_ref[...] + pv
        m_ref[...] = m_new

    lanep = lax.broadcasted_iota(jnp.int32, (1, PAGE_SIZE), 1)
    s_parts = []
    v_list = []
    for r in range(npp):
        kb = k_refs[r][...].astype(BF16)
        kpos = ((j * npp + r) * PAGE_SIZE + lanep).astype(F32)
        s_parts.append(_dot_nt(qbd, kb) - slope * ((past + tq) - kpos))
        v_list.append(v_refs[r][...].astype(BF16))
    update(jnp.concatenate(s_parts, axis=1), v_list)

    @pl.when(j == nj - 1)
    def _():
        zpad = jnp.zeros((PAGE_SIZE - GROUP, knew_ref.shape[1]), F32)
        kb = jnp.concatenate([knew_ref[...], zpad], axis=0).astype(BF16)
        vb = jnp.concatenate([vnew_ref[...], zpad], axis=0).astype(BF16)
        tk = (lanep - TOK0).astype(F32)
        s = _dot_nt(qbd, kb) - slope * (tq - tk)
        ok = jnp.logical_and(jnp.logical_and(lanep >= TOK0, lanep < GROUP), tk <= tq)
        update(jnp.where(ok, s, -jnp.inf), [vb])

        lam = _lambda_full(lam_ref)
        linv = 1.0 / l_ref[...]
        rows8 = lax.broadcasted_iota(jnp.int32, (GROUP, 1), 0)
        for h in range(ATT_HEADS):
            blk = acc_ref[h * GROUP:(h + 1) * GROUP, h * ATT_V_DIM:(h + 1) * ATT_V_DIM] \
                * linv[h * GROUP:(h + 1) * GROUP, :]
            o = blk - lam * pltpu.roll(blk, TOK0, axis=0)
            o = _rms(o, sub_ref[...]) * (1.0 - LAMBDA_INIT)
            o_ref[:, h * ATT_V_DIM:(h + 1) * ATT_V_DIM] = jnp.where(rows8 >= TOK0, o, 0.0).astype(o_ref.dtype)


def attn_sample(page_table, cache_k, cache_v, qbd, k_new, v_new, lam4, subln, npp):
    nb, n_pages = page_table.shape
    n_pool = cache_k.shape[0]
    dk = ATT_HEADS * 2 * ATT_HEAD_DIM
    ck = cache_k.reshape(n_pool, PAGE_SIZE, dk)
    cv = cache_v.reshape(n_pool, PAGE_SIZE, dk)
    nr = qbd.shape[1]
    slopes = jnp.asarray(np.tile(np.repeat(_alibi_slopes(), GROUP)[:, None], (1, LANES)))

    def page_spec(r):
        return pl.BlockSpec((None, PAGE_SIZE, dk), lambda b, j, pt: (pt[b, j * npp + r], 0, 0))

    in_specs = [page_spec(r) for r in range(npp)] + [page_spec(r) for r in range(npp)]
    in_specs += [pl.BlockSpec((None, nr, dk), lambda b, j, pt: (b, 0, 0)),
                 pl.BlockSpec((GROUP, dk), lambda b, j, pt: (b, 0)),
                 pl.BlockSpec((GROUP, dk), lambda b, j, pt: (b, 0)),
                 pl.BlockSpec(slopes.shape, lambda b, j, pt: (0, 0)),
                 pl.BlockSpec(lam4.shape, lambda b, j, pt: (0, 0)),
                 pl.BlockSpec(subln.shape, lambda b, j, pt: (0, 0))]
    grid_spec = pltpu.PrefetchScalarGridSpec(
        num_scalar_prefetch=1,
        grid=(nb, n_pages // npp),
        in_specs=in_specs,
        out_specs=pl.BlockSpec((GROUP, dk), lambda b, j, pt: (b, 0)),
        scratch_shapes=[pltpu.VMEM((nr, 1), F32), pltpu.VMEM((nr, 1), F32), pltpu.VMEM((nr, dk), F32)],
    )
    return pl.pallas_call(
        functools.partial(_attn_sample_kernel, npp=npp, past=float(n_pages * PAGE_SIZE)),
        grid_spec=grid_spec,
        out_shape=jax.ShapeDtypeStruct((nb * GROUP, dk), BF16),
        compiler_params=_cparams(("parallel", "arbitrary")),
        name="attn_sample",
    )(page_table, *([ck] * npp), *([cv] * npp), qbd, k_new, v_new, slopes, lam4, subln)


def _block_diag4(w):
    nb = w.shape[0] // 4
    eye = jnp.eye(4, dtype=w.dtype)
    w4 = w.reshape(nb, 4, RNN_BW, RNN_BW)
    return jnp.einsum('cajk,ab->cajbk', w4, eye).reshape(nb, 4 * RNN_BW, 4 * RNN_BW).astype(BF16)


def _prep_weights(norm_mix, norm_ffn, norm_final, w_in0, conv_rnn_w, conv_rnn_b, rg_w_a, rg_b_a, rg_w_x, rg_b_x,
                  rg_lambda, conv_ssm_w, conv_ssm_b, dt_bias, a_log, d_skip, ssm_norm, w_out0, w_in1, lambda_q1,
                  lambda_k1, lambda_q2, lambda_k2, subln, w_out1, w_up, ffn_conv_w, ffn_conv_b, w_down):
    row = lambda v: v.reshape(1, -1)
    pad_l = lambda v: jnp.pad(v.reshape(1, -1), ((0, 0), (0, LANES - v.shape[-1])))
    return dict(
        norm_mix=[row(norm_mix[l]) for l in range(2)],
        norm_ffn=[row(norm_ffn[l]) for l in range(2)],
        norm_final=row(norm_final),
        w_in0=jnp.pad(w_in0, ((0, 0), (0, D_IN0_PAD - D_IN0))).astype(BF16),
        conv_rnn_w=conv_rnn_w, conv_rnn_b=row(conv_rnn_b),
        wa=_block_diag4(rg_w_a), ba=row(rg_b_a), wx=_block_diag4(rg_w_x), bx=row(rg_b_x), lam=row(rg_lambda),
        conv_ssm_w=conv_ssm_w, conv_ssm_b=row(conv_ssm_b),
        dt_bias=pad_l(dt_bias), a_log=pad_l(a_log),
        d_skip=row(jnp.repeat(d_skip, SSM_HEAD_DIM)), ssm_norm=row(ssm_norm),
        w_out0a=w_out0[:D_RNN].astype(BF16), w_out0b=w_out0[D_RNN:].astype(BF16),
        w_in1=w_in1.astype(BF16),
        lam4=jnp.stack([lambda_q1, lambda_k1, lambda_q2, lambda_k2]), subln=row(subln),
        w_out1=w_out1.astype(BF16),
        w_up=[w_up[l].astype(BF16) for l in range(2)],
        ffn_conv_w=[ffn_conv_w[l] for l in range(2)], ffn_conv_b=[row(ffn_conv_b[l]) for l in range(2)],
        w_down=[w_down[l].astype(BF16) for l in range(2)],
    )


def _state_t(s):
    b = s.shape[0]
    return jnp.transpose(s, (0, 3, 1, 2)).reshape(b, SSM_STATE, D_SSM)


def _state_untranspose(st):
    b = st.shape[0]
    return jnp.transpose(st.reshape(b, SSM_STATE, SSM_HEADS, SSM_HEAD_DIM), (0, 2, 3, 1))


def _prompt_trunk(x_prompt, w):
    bsz, seq, d = x_prompt.shape
    m = bsz * seq
    x = x_prompt.reshape(m, d)
    tm = min(512, seq)
    tr = min(256, seq)

    p0 = proj0(x, w['norm_mix'][0], w['w_in0'], tm)
    rnn_out, h_tail = rglru(p0, bsz, seq, tr, w['conv_rnn_w'], w['conv_rnn_b'], w['wa'], w['ba'], w['wx'],
                            w['bx'], w['lam'])
    lc = SSD_CHUNK
    nch = seq // lc
    hb = lc // HALO
    xbc_cb = (2 * D_RNN + D_SSM) // SSM_CONV_DIM
    halo_spec = pl.BlockSpec(
        (HALO, SSM_CONV_DIM), lambda b, i: (jnp.maximum((b * nch + i) * hb - 1, 0), xbc_cb))
    y_ssm, st = ssd(p0, xbc_cb, p0, halo_spec, p0, 2 * D_RNN // D_SSM, p0, (D_IN0_PAD - LANES) // LANES,
                    bsz, nch, lc, lc, w['conv_ssm_w'], w['conv_ssm_b'], w['dt_bias'], w['a_log'],
                    w['d_skip'], w['ssm_norm'])
    x = out_proj([rnn_out, y_ssm], [w['w_out0a'], w['w_out0b']], x, tm)
    x, u0 = ffn(x, bsz, seq, tm, w['norm_ffn'][0], w['w_up'][0], w['ffn_conv_w'][0], w['ffn_conv_b'][0],
                w['w_down'][0], w['norm_final'], False)

    k, v, qkv = proj1(x, w['norm_mix'][1], w['w_in1'], tm)
    o = attn_prompt(qkv, bsz, seq, min(512, seq), w['lam4'], w['subln'])
    x = out_proj([o], [w['w_out1']], x, tm)
    y, u1 = ffn(x, bsz, seq, tm, w['norm_ffn'][1], w['w_up'][1], w['ffn_conv_w'][1], w['ffn_conv_b'][1],
                w['w_down'][1], w['norm_final'], True)

    p3 = p0.reshape(bsz, seq, D_IN0_PAD)
    rnn_conv = p3[:, seq - (CONV_W - 1):, 0:D_RNN]
    ssm_conv = p3[:, seq - (CONV_W - 1):, 2 * D_RNN + D_SSM:2 * D_RNN + D_SSM + SSM_CONV_DIM]
    ffn_conv = jnp.stack([u0[:, SUBLANES - (FFN_CONV_W - 1):], u1[:, SUBLANES - (FFN_CONV_W - 1):]])
    return (y.reshape(bsz, seq, d), rnn_conv, h_tail[:, SUBLANES - 1], ssm_conv, _state_untranspose(st),
            k.reshape(bsz, seq, ATT_HEADS, 2 * ATT_HEAD_DIM), v.reshape(bsz, seq, ATT_HEADS, ATT_V_DIM), ffn_conv)


def _to_groups(x_tok, hist=None):
    b, t, c = x_tok.shape
    if hist is None:
        lead = jnp.zeros((b, TOK0, c), x_tok.dtype)
    else:
        k = hist.shape[1]
        lead = jnp.concatenate([jnp.zeros((b, TOK0 - k, c), x_tok.dtype), hist], axis=1)
    return jnp.concatenate([lead, x_tok], axis=1).reshape(b * GROUP, c)


def _sample_trunk(x_sample, state_rnn_conv, state_rnn_h, state_ssm_conv, state_ssm, cache_k, cache_v,
                  state_ffn_conv, page_table, w):
    nb, t, d = x_sample.shape
    assert t == GROUP - TOK0
    m = nb * GROUP
    x = _to_groups(x_sample)

    p0 = proj0(x, w['norm_mix'][0], w['w_in0'], m)
    p0g = p0.reshape(nb, GROUP, D_IN0_PAD)
    c_xbc = 2 * D_RNN + D_SSM
    xr = _to_groups(p0g[:, TOK0:, 0:D_RNN], state_rnn_conv)
    p0r = jnp.concatenate([xr, p0[:, D_RNN:2 * D_RNN]], axis=1)
    hinj = _to_groups(jnp.zeros((nb, t, D_RNN), F32), state_rnn_h[:, None, :])
    rnn_out, h_all = rglru(p0r, 1, m, m, w['conv_rnn_w'], w['conv_rnn_b'], w['wa'], w['ba'], w['wx'], w['bx'],
                           w['lam'], hinj=hinj)
    lc = SSD_CHUNK
    padc = lambda a: jnp.pad(a, ((0, 0), (0, lc - t), (0, 0))).reshape(nb * lc, a.shape[-1])
    xbc_p = padc(p0g[:, TOK0:, c_xbc:c_xbc + SSM_CONV_DIM])
    z_p = padc(p0g[:, TOK0:, 2 * D_RNN:2 * D_RNN + D_SSM])
    dt_p = padc(p0g[:, TOK0:, D_IN0_PAD - LANES:])
    halo = jnp.concatenate([jnp.zeros((nb, HALO - (CONV_W - 1), SSM_CONV_DIM), F32), state_ssm_conv], axis=1)
    halo_spec = pl.BlockSpec((None, HALO, SSM_CONV_DIM), lambda b, i: (b, 0, 0))
    y_p, st = ssd(xbc_p, 0, halo, halo_spec, z_p, 0, dt_p, 0, nb, 1, lc, t, w['conv_ssm_w'], w['conv_ssm_b'],
                  w['dt_bias'], w['a_log'], w['d_skip'], w['ssm_norm'], s0=_state_t(state_ssm))
    y_ssm = _to_groups(y_p.reshape(nb, lc, D_SSM)[:, 0:t])
    x = out_proj([rnn_out, y_ssm], [w['w_out0a'], w['w_out0b']], x, m)
    uinj0 = _to_groups(jnp.zeros((nb, t, 2 * D_FF), F32), state_ffn_conv[0])
    x, u0 = ffn(x, 1, m, m, w['norm_ffn'][0], w['w_up'][0], w['ffn_conv_w'][0], w['ffn_conv_b'][0],
                w['w_down'][0], w['norm_final'], False, uinj=uinj0)

    k, v, qkv = proj1(x, w['norm_mix'][1], w['w_in1'], m)
    qg = qkv[:, 0:D_MODEL].reshape(nb, GROUP, ATT_HEADS, 2, ATT_HEAD_DIM)[:, TOK0:]
    eye_h = jnp.eye(ATT_HEADS, dtype=BF16)
    flip = jnp.array([[0, 1], [1, 0]], dtype=BF16)
    qbd = jnp.einsum('bqhmd,hg,nm->bhnqgmd', qg, eye_h, flip).reshape(nb, ATT_HEADS * GROUP, D_MODEL)
    o = attn_sample(page_table, cache_k, cache_v, qbd, k, v, w['lam4'], w['subln'], npp=8)
    x = out_proj([o], [w['w_out1']], x, m)
    uinj1 = _to_groups(jnp.zeros((nb, t, 2 * D_FF), F32), state_ffn_conv[1])
    y, u1 = ffn(x, 1, m, m, w['norm_ffn'][1], w['w_up'][1], w['ffn_conv_w'][1], w['ffn_conv_b'][1],
                w['w_down'][1], w['norm_final'], True, uinj=uinj1)

    tok = lambda a: a.reshape(nb, GROUP, a.shape[-1])[:, TOK0:]
    rnn_conv = p0g[:, GROUP - (CONV_W - 1):, 0:D_RNN]
    ssm_conv = p0g[:, GROUP - (CONV_W - 1):, c_xbc:c_xbc + SSM_CONV_DIM]
    ffn_conv = jnp.stack([u0.reshape(nb, GROUP, -1)[:, GROUP - (FFN_CONV_W - 1):],
                          u1.reshape(nb, GROUP, -1)[:, GROUP - (FFN_CONV_W - 1):]])
    return (tok(y), rnn_conv, h_all.reshape(nb, GROUP, D_RNN)[:, GROUP - 1], ssm_conv, _state_untranspose(st),
            tok(k).reshape(nb, t, ATT_HEADS, 2 * ATT_HEAD_DIM), tok(v).reshape(nb, t, ATT_HEADS, ATT_V_DIM),
            ffn_conv)


def kernel(x_prompt, x_sample, state_rnn_conv, state_rnn_h, state_ssm_conv, state_ssm, cache_k, cache_v, state_ffn_conv, page_table, norm_mix, norm_ffn, norm_final, w_in0, conv_rnn_w, conv_rnn_b, rg_w_a, rg_b_a, rg_w_x, rg_b_x, rg_lambda, conv_ssm_w, conv_ssm_b, dt_bias, a_log, d_skip, ssm_norm, w_out0, w_in1, lambda_q1, lambda_k1, lambda_q2, lambda_k2, subln, w_out1, w_up, ffn_conv_w, ffn_conv_b, w_down):
    w = _prep_weights(norm_mix, norm_ffn, norm_final, w_in0, conv_rnn_w, conv_rnn_b, rg_w_a, rg_b_a, rg_w_x,
                      rg_b_x, rg_lambda, conv_ssm_w, conv_ssm_b, dt_bias, a_log, d_skip, ssm_norm, w_out0, w_in1,
                      lambda_q1, lambda_k1, lambda_q2, lambda_k2, subln, w_out1, w_up, ffn_conv_w, ffn_conv_b,
                      w_down)
    p = _prompt_trunk(x_prompt, w)
    s = _sample_trunk(x_sample, state_rnn_conv, state_rnn_h, state_ssm_conv, state_ssm, cache_k, cache_v,
                      state_ffn_conv, page_table, w)
    return (p[0], s[0]) + p[1:] + s[1:]
```

```python
import functools
import math

import jax
import jax.numpy as jnp
import numpy as np
from jax import lax
from jax.experimental import pallas as pl
from jax.experimental.pallas import tpu as pltpu

F32 = jnp.float32
BF16 = jnp.bfloat16

D_MODEL = 1024
D_RNN = 1024
RNN_BW = 64
RG_C = 8.0
CONV_W = 4
SSM_HEADS = 16
SSM_HEAD_DIM = 64
D_SSM = SSM_HEADS * SSM_HEAD_DIM
SSM_GROUPS = 2
SSM_STATE = 128
SSM_CONV_DIM = D_SSM + 2 * SSM_GROUPS * SSM_STATE
SSD_CHUNK = 128
D_IN0 = 2 * D_RNN + D_SSM + SSM_CONV_DIM + SSM_HEADS
D_IN0_PAD = 2 * D_RNN + D_SSM + SSM_CONV_DIM + 128
ATT_HEADS = 8
ATT_HEAD_DIM = 64
ATT_V_DIM = 128
LAMBDA_INIT = 0.8 - 0.6 * math.exp(-0.3 * 1)
D_FF = 3 * D_MODEL
FFN_CONV_W = 3
EPS = 1e-6
PAGE_SIZE = 128

SUBLANES = 8
LANES = 128
HALO = SUBLANES
GROUP = 8
TOK0 = 4
VMEM_LIMIT = 56 * 1024 * 1024


def _cparams(sem):
    return pltpu.CompilerParams(dimension_semantics=sem, vmem_limit_bytes=VMEM_LIMIT)


def _resident(shape):
    nd = len(shape)
    return pl.BlockSpec(shape, lambda *_: (0,) * nd, pipeline_mode=pl.Buffered(1))


def _rms(x, g):
    var = jnp.mean(x * x, axis=-1, keepdims=True)
    return x * lax.rsqrt(var + EPS) * g


def _softplus(x):
    return jnp.maximum(x, 0.0) + jnp.log1p(jnp.exp(-jnp.abs(x)))


def _dot(a, b):
    return jnp.dot(a, b, preferred_element_type=F32)


def _dot_nt(a, b):
    return lax.dot_general(a, b, (((1,), (1,)), ((), ())), preferred_element_type=F32)


def _proj0_kernel(x_ref, g_ref, w_ref, o_ref, *, tn):
    xn = _rms(x_ref[...], g_ref[...]).astype(BF16)
    n = w_ref.shape[1]
    for c in range(0, n, tn):
        w = min(tn, n - c)
        o_ref[:, c:c + w] = _dot(xn, w_ref[:, c:c + w])


def proj0(x, g, w, tm):
    m, d = x.shape
    n = w.shape[1]
    return pl.pallas_call(
        functools.partial(_proj0_kernel, tn=512),
        grid=(m // tm,),
        in_specs=[pl.BlockSpec((tm, d), lambda i: (i, 0)), _resident((1, d)), _resident((d, n))],
        out_specs=pl.BlockSpec((tm, n), lambda i: (i, 0)),
        out_shape=jax.ShapeDtypeStruct((m, n), F32),
        compiler_params=_cparams(("parallel",)),
        name="proj0",
    )(x, g, w)


def _proj1_kernel(x_ref, g_ref, w_ref, k_ref, v_ref, qkv_ref, *, tn):
    xn = _rms(x_ref[...], g_ref[...]).astype(BF16)
    d = k_ref.shape[1]
    for c in range(0, 3 * d, tn):
        r = _dot(xn, w_ref[:, c:c + tn])
        if c < d:
            qkv_ref[:, c:c + tn] = (r * (ATT_HEAD_DIM ** -0.5)).astype(BF16)
        else:
            qkv_ref[:, c:c + tn] = r.astype(BF16)
            if c < 2 * d:
                k_ref[:, c - d:c - d + tn] = r
            else:
                v_ref[:, c - 2 * d:c - 2 * d + tn] = r


def proj1(x, g, w, tm):
    m, d = x.shape
    n = w.shape[1]
    return pl.pallas_call(
        functools.partial(_proj1_kernel, tn=512),
        grid=(m // tm,),
        in_specs=[pl.BlockSpec((tm, d), lambda i: (i, 0)), _resident((1, d)), _resident((d, n))],
        out_specs=[pl.BlockSpec((tm, d), lambda i: (i, 0)), pl.BlockSpec((tm, d), lambda i: (i, 0)),
                   pl.BlockSpec((tm, n), lambda i: (i, 0))],
        out_shape=[jax.ShapeDtypeStruct((m, d), F32), jax.ShapeDtypeStruct((m, d), F32),
                   jax.ShapeDtypeStruct((m, n), BF16)],
        compiler_params=_cparams(("parallel",)),
        name="proj1",
    )(x, g, w)


def _out_proj_kernel(*refs, n_in, tn):
    a_refs = refs[:n_in]
    w_refs = refs[n_in:2 * n_in]
    res_ref = refs[2 * n_in]
    o_ref = refs[2 * n_in + 1]
    n = o_ref.shape[1]
    for c in range(0, n, tn):
        acc = res_ref[:, c:c + tn]
        for a_ref, w_ref in zip(a_refs, w_refs):
            acc = acc + _dot(a_ref[...], w_ref[:, c:c + tn])
        o_ref[:, c:c + tn] = acc


def out_proj(a_list, w_list, res, tm):
    m, n = res.shape
    n_in = len(a_list)
    in_specs = [pl.BlockSpec((tm, a.shape[1]), lambda i: (i, 0)) for a in a_list]
    in_specs += [_resident(w.shape) for w in w_list]
    in_specs += [pl.BlockSpec((tm, n), lambda i: (i, 0))]
    return pl.pallas_call(
        functools.partial(_out_proj_kernel, n_in=n_in, tn=512),
        grid=(m // tm,),
        in_specs=in_specs,
        out_specs=pl.BlockSpec((tm, n), lambda i: (i, 0)),
        out_shape=jax.ShapeDtypeStruct((m, n), F32),
        compiler_params=_cparams(("parallel",)),
        name="out_proj",
    )(*a_list, *w_list, res)


def _conv_from_ext(ext_ref, w_ref, b_ref, width, tm, c0, c1):
    y = b_ref[:, c0:c1]
    for j in range(width):
        off = HALO - (width - 1) + j
        y = y + ext_ref[off:off + tm, c0:c1] * w_ref[j:j + 1, c0:c1]
    return y


def _scan_rows(a, b, c):
    tm = a.shape[0]
    rowm = lax.broadcasted_iota(jnp.int32, a.shape, 0) % SUBLANES
    s = 1
    while s < SUBLANES:
        keep = rowm >= s
        a_s = jnp.where(keep, pltpu.roll(a, s, axis=0), 1.0)
        b_s = jnp.where(keep, pltpu.roll(b, s, axis=0), 0.0)
        b = a * b_s + b
        a = a * a_s
        s *= 2
    hs = []
    for g in range(tm // SUBLANES):
        hg = a[g * SUBLANES:(g + 1) * SUBLANES] * c + b[g * SUBLANES:(g + 1) * SUBLANES]
        c = hg[SUBLANES - 1:SUBLANES, :]
        hs.append(hg)
    return jnp.concatenate(hs, axis=0)


def _rglru_kernel(*refs, tm, sample):
    if sample:
        (x_ref, y_ref, cw_ref, cb_ref, wa_ref, ba_ref, wx_ref, bx_ref, lam_ref, hinj_ref,
         out_ref, h_out_ref, ext_ref, carry_ref) = refs
    else:
        (x_ref, y_ref, halo_ref, cw_ref, cb_ref, wa_ref, ba_ref, wx_ref, bx_ref, lam_ref,
         out_ref, h_out_ref, ext_ref, carry_ref) = refs
    i = pl.program_id(1)
    d = x_ref.shape[1]

    @pl.when(i == 0)
    def _():
        carry_ref[...] = jnp.zeros_like(carry_ref)

    if sample:
        ext_ref[0:HALO, :] = jnp.zeros((HALO, d), F32)
    else:
        ext_ref[0:HALO, :] = jnp.where(i == 0, 0.0, halo_ref[...])
    ext_ref[HALO:HALO + tm, :] = x_ref[...]

    row = lax.broadcasted_iota(jnp.int32, (tm, 1), 0)
    tail = h_out_ref.shape[0]
    sp = _softplus(-lam_ref[...])
    cbw = 4 * RNN_BW
    for c in range(d // cbw):
        c0, c1 = c * cbw, (c + 1) * cbw
        xr = _conv_from_ext(ext_ref, cw_ref, cb_ref, CONV_W, tm, c0, c1)
        xb = xr.astype(BF16)
        r = jax.nn.sigmoid(_dot(xb, wa_ref[c]) + ba_ref[:, c0:c1])
        ig = jax.nn.sigmoid(_dot(xb, wx_ref[c]) + bx_ref[:, c0:c1])
        log_a = -RG_C * r * sp[:, c0:c1]
        a = jnp.exp(log_a)
        mult = jnp.sqrt(jnp.maximum(1.0 - a * a, 0.0))
        if sample:
            inj = (row % GROUP) == (TOK0 - 1)
            a = jnp.where(inj, 0.0, a)
            bv = jnp.where(inj, hinj_ref[:, c0:c1], xr * ig * mult)
        else:
            first = jnp.logical_and(i == 0, row == 0)
            a = jnp.where(first, 0.0, a)
            mult = jnp.where(first, 1.0, mult)
            bv = xr * ig * mult
        for k in range(cbw // LANES):
            l0, l1 = c0 + k * LANES, c0 + (k + 1) * LANES
            h = _scan_rows(a[:, k * LANES:(k + 1) * LANES], bv[:, k * LANES:(k + 1) * LANES], carry_ref[:, l0:l1])
            carry_ref[:, l0:l1] = h[tm - 1:tm, :]
            out_ref[:, l0:l1] = (h * jax.nn.gelu(y_ref[:, l0:l1])).astype(out_ref.dtype)
            h_out_ref[:, l0:l1] = h[tm - tail:tm, :]


def rglru(proj, nseq, seq, tm, cw, cb, wa, ba, wx, bx, lam, hinj=None):
    sample = hinj is not None
    nt = seq // tm
    d = D_RNN
    xmap = lambda b, i: (b * nt + i, 0)
    in_specs = [pl.BlockSpec((tm, d), xmap), pl.BlockSpec((tm, d), lambda b, i: (b * nt + i, 1))]
    args = [proj, proj]
    if not sample:
        hb = tm // HALO
        in_specs.append(pl.BlockSpec((HALO, d), lambda b, i: (jnp.maximum((b * nt + i) * hb - 1, 0), 0)))
        args.append(proj)
    in_specs += [_resident(cw.shape), _resident(cb.shape), _resident(wa.shape), _resident(ba.shape),
                 _resident(wx.shape), _resident(bx.shape), _resident(lam.shape)]
    args += [cw, cb, wa, ba, wx, bx, lam]
    if sample:
        in_specs.append(pl.BlockSpec((tm, d), xmap))
        args.append(hinj)
        h_spec = pl.BlockSpec((tm, d), xmap)
        h_shape = jax.ShapeDtypeStruct((nseq * seq, d), F32)
    else:
        h_spec = pl.BlockSpec((None, SUBLANES, d), lambda b, i: (b, 0, 0))
        h_shape = jax.ShapeDtypeStruct((nseq, SUBLANES, d), F32)
    return pl.pallas_call(
        functools.partial(_rglru_kernel, tm=tm, sample=sample),
        grid=(nseq, nt),
        in_specs=in_specs,
        out_specs=[pl.BlockSpec((tm, d), xmap), h_spec],
        out_shape=[jax.ShapeDtypeStruct((nseq * seq, d), BF16), h_shape],
        scratch_shapes=[pltpu.VMEM((tm + HALO, d), F32), pltpu.VMEM((1, d), F32)],
        compiler_params=_cparams(("parallel", "arbitrary")),
        name="rglru",
    )(*args)


def _pair(v, hp, lane_lo):
    return jnp.where(lane_lo, v[:, 2 * hp:2 * hp + 1], v[:, 2 * hp + 1:2 * hp + 2])


def _ssd_kernel(*refs, lc, n_valid, has_init):
    if has_init:
        (xbc_ref, halo_ref, z_ref, dt_ref, cw_ref, cb_ref, dtb_ref, alog_ref, dsk_ref, nrm_ref, s0_ref,
         y_ref, sout_ref, ext_ref, st_ref, yacc_ref) = refs
    else:
        (xbc_ref, halo_ref, z_ref, dt_ref, cw_ref, cb_ref, dtb_ref, alog_ref, dsk_ref, nrm_ref,
         y_ref, sout_ref, ext_ref, st_ref, yacc_ref) = refs
    i = pl.program_id(1)
    n = SSM_STATE
    hd2 = 2 * SSM_HEAD_DIM
    gw = D_SSM // SSM_GROUPS

    @pl.when(i == 0)
    def _():
        if has_init:
            st_ref[...] = s0_ref[...]
        else:
            st_ref[...] = jnp.zeros_like(st_ref)

    if has_init:
        ext_ref[0:HALO, :] = halo_ref[...]
    else:
        ext_ref[0:HALO, :] = jnp.where(i == 0, 0.0, halo_ref[...])
    ext_ref[HALO:HALO + lc, :] = xbc_ref[...]
    xbc = _conv_from_ext(ext_ref, cw_ref, cb_ref, CONV_W, lc, 0, SSM_CONV_DIM)
    xbc = xbc * jax.nn.sigmoid(xbc)
    xs = xbc[:, 0:D_SSM]

    row = lax.broadcasted_iota(jnp.int32, (lc, 1), 0)
    dt = _softplus(dt_ref[...] + dtb_ref[...])
    if n_valid < lc:
        dt = jnp.where(row < n_valid, dt, 0.0)
    acs = dt * (-jnp.exp(alog_ref[...]))
    s = 1
    while s < lc:
        acs = acs + jnp.where(row >= s, pltpu.roll(acs, s, axis=0), 0.0)
        s *= 2
    if lc == LANES:
        acs_t = acs.T
    else:
        acs_t = jnp.concatenate([acs, jnp.zeros((LANES - lc, LANES), F32)], axis=0).T[:, 0:lc]
    a_last = acs[lc - 1:lc, :]
    ea = jnp.exp(acs)
    dsv = jnp.exp(a_last - acs)
    cdec = jnp.exp(a_last)

    lane_lo = lax.broadcasted_iota(jnp.int32, (1, hd2), 1) < SSM_HEAD_DIM
    causal = (lax.broadcasted_iota(jnp.int32, (lc, lc), 0) >= lax.broadcasted_iota(jnp.int32, (lc, lc), 1))
    for g in range(SSM_GROUPS):
        bm = xbc[:, D_SSM + g * n:D_SSM + (g + 1) * n]
        cm = xbc[:, D_SSM + SSM_GROUPS * n + g * n:D_SSM + SSM_GROUPS * n + (g + 1) * n]
        bm_b = bm.astype(BF16)
        cm_b = cm.astype(BF16)
        cbm = _dot_nt(cm_b, bm_b)
        st_old = st_ref[:, g * gw:(g + 1) * gw]
        yoff = _dot(cm_b, st_old.astype(BF16))
        if lc == LANES:
            bm_t = bm.T.astype(BF16)
        else:
            bm_t = jnp.concatenate([bm, jnp.zeros((LANES - lc, n), F32)], axis=0).T[:, 0:lc].astype(BF16)
        w_parts = []
        cd_parts = []
        for pp in range(gw // hd2):
            hp = g * (gw // hd2) + pp
            h0, h1 = 2 * hp, 2 * hp + 1
            c0 = hp * hd2
            xg = xs[:, c0:c0 + hd2] * _pair(dt, hp, lane_lo)
            xg_b = xg.astype(BF16)
            yd = []
            for h in (h0, h1):
                seg = acs[:, h:h + 1] - acs_t[h:h + 1, :]
                dec = jnp.exp(jnp.where(causal, seg, -jnp.inf))
                yd.append(_dot((cbm * dec).astype(BF16), xg_b))
            ydiag = jnp.where(lane_lo, yd[0], yd[1])
            yo = yoff[:, pp * hd2:(pp + 1) * hd2] * _pair(ea, hp, lane_lo)
            yacc_ref[:, c0:c0 + hd2] = ydiag + yo + dsk_ref[:, c0:c0 + hd2] * xs[:, c0:c0 + hd2]
            w_parts.append((xg * _pair(dsv, hp, lane_lo)).astype(BF16))
            cd_parts.append(_pair(cdec, hp, lane_lo))
        w_all = jnp.concatenate(w_parts, axis=1)
        cd_all = jnp.concatenate(cd_parts, axis=1)
        st_ref[:, g * gw:(g + 1) * gw] = cd_all * st_old + _dot(bm_t, w_all)

    z = z_ref[...]
    y = yacc_ref[...] * (z * jax.nn.sigmoid(z))
    y_ref[...] = _rms(y, nrm_ref[...]).astype(y_ref.dtype)
    sout_ref[...] = st_ref[...]


def ssd(xbc_arr, xbc_col, halo_arr, halo_map, z_arr, z_col, dt_arr, dt_col, nseq, nchunk, lc, n_valid,
        cw, cb, dtb, alog, dsk, nrm, s0=None):
    has_init = s0 is not None
    rmap = lambda col: (lambda b, i: (b * nchunk + i, col))
    in_specs = [pl.BlockSpec((lc, SSM_CONV_DIM), rmap(xbc_col)),
                halo_map,
                pl.BlockSpec((lc, D_SSM), rmap(z_col)),
                pl.BlockSpec((lc, LANES), rmap(dt_col)),
                _resident(cw.shape), _resident(cb.shape), _resident(dtb.shape), _resident(alog.shape),
                _resident(dsk.shape), _resident(nrm.shape)]
    args = [xbc_arr, halo_arr, z_arr, dt_arr, cw, cb, dtb, alog, dsk, nrm]
    if has_init:
        in_specs.append(pl.BlockSpec((None, SSM_STATE, D_SSM), lambda b, i: (b, 0, 0)))
        args.append(s0)
    return pl.pallas_call(
        functools.partial(_ssd_kernel, lc=lc, n_valid=n_valid, has_init=has_init),
        grid=(nseq, nchunk),
        in_specs=in_specs,
        out_specs=[pl.BlockSpec((lc, D_SSM), lambda b, i: (b * nchunk + i, 0)),
                   pl.BlockSpec((None, SSM_STATE, D_SSM), lambda b, i: (b, 0, 0))],
        out_shape=[jax.ShapeDtypeStruct((nseq * nchunk * lc, D_SSM), BF16),
                   jax.ShapeDtypeStruct((nseq, SSM_STATE, D_SSM), F32)],
        scratch_shapes=[pltpu.VMEM((lc + HALO, SSM_CONV_DIM), F32), pltpu.VMEM((SSM_STATE, D_SSM), F32),
                        pltpu.VMEM((lc, D_SSM), F32)],
        compiler_params=_cparams(("parallel", "arbitrary")),
        name="ssd",
    )(*args)


def _ffn_kernel(*refs, tm, fc, sample, final_norm):
    if sample:
        (x_ref, g_ref, wup_ref, cw_ref, cb_ref, wdn_ref, gf_ref, uinj_ref,
         o_ref, u_out_ref, ext_ref) = refs
    else:
        (x_ref, halo_ref, g_ref, wup_ref, cw_ref, cb_ref, wdn_ref, gf_ref,
         o_ref, u_out_ref, ext_ref) = refs
    i = pl.program_id(1)
    d_ff = wdn_ref.shape[0]
    x = x_ref[...]
    hn = _rms(x, g_ref[...]).astype(BF16)
    if sample:
        row = lax.broadcasted_iota(jnp.int32, (tm, 1), 0) % GROUP
        hist = jnp.logical_and(row >= TOK0 - (FFN_CONV_W - 1), row < TOK0)
    else:
        halo_n = _rms(halo_ref[...], g_ref[...]).astype(BF16)
    tail = u_out_ref.shape[0]
    acc = x
    for c in range(0, d_ff, fc):
        conv = []
        for base in (c, d_ff + c):
            u = _dot(hn, wup_ref[:, base:base + fc])
            if sample:
                u = jnp.where(hist, uinj_ref[:, base:base + fc], u)
                ext_ref[0:HALO, :] = jnp.zeros((HALO, fc), F32)
            else:
                uh = _dot(halo_n, wup_ref[:, base:base + fc])
                ext_ref[0:HALO, :] = jnp.where(i == 0, 0.0, uh)
            ext_ref[HALO:HALO + tm, :] = u
            u_out_ref[:, base:base + fc] = ext_ref[HALO + tm - tail:HALO + tm, :]
            y = cb_ref[:, base:base + fc]
            for j in range(FFN_CONV_W):
                off = HALO - (FFN_CONV_W - 1) + j
                y = y + ext_ref[off:off + tm, :] * cw_ref[j:j + 1, base:base + fc]
            conv.append(y)
        act = (jax.nn.gelu(conv[0]) * conv[1]).astype(BF16)
        acc = acc + _dot(act, wdn_ref[c:c + fc, :])
    if final_norm:
        acc = _rms(acc, gf_ref[...])
    o_ref[...] = acc


def ffn(x, nseq, seq, tm, g, wup, cw, cb, wdn, gf, final_norm, uinj=None):
    sample = uinj is not None
    nt = seq // tm
    d = x.shape[1]
    d_up = wup.shape[1]
    xmap = lambda b, i: (b * nt + i, 0)
    in_specs = [pl.BlockSpec((tm, d), xmap)]
    args = [x]
    if not sample:
        hb = tm // HALO
        in_specs.append(pl.BlockSpec((HALO, d), lambda b, i: (jnp.maximum((b * nt + i) * hb - 1, 0), 0)))
        args.append(x)
    in_specs += [_resident(g.shape), _resident(wup.shape), _resident(cw.shape), _resident(cb.shape),
                 _resident(wdn.shape), _resident(gf.shape)]
    args += [g, wup, cw, cb, wdn, gf]
    if sample:
        in_specs.append(pl.BlockSpec((tm, d_up), xmap))
        args.append(uinj)
        u_spec = pl.BlockSpec((tm, d_up), xmap)
        u_shape = jax.ShapeDtypeStruct((nseq * seq, d_up), F32)
    else:
        u_spec = pl.BlockSpec((None, SUBLANES, d_up), lambda b, i: (b, 0, 0))
        u_shape = jax.ShapeDtypeStruct((nseq, SUBLANES, d_up), F32)
    fc = 512
    return pl.pallas_call(
        functools.partial(_ffn_kernel, tm=tm, fc=fc, sample=sample, final_norm=final_norm),
        grid=(nseq, nt),
        in_specs=in_specs,
        out_specs=[pl.BlockSpec((tm, d), xmap), u_spec],
        out_shape=[jax.ShapeDtypeStruct((nseq * seq, d), F32), u_shape],
        scratch_shapes=[pltpu.VMEM((tm + HALO, fc), F32)],
        compiler_params=_cparams(("parallel", "arbitrary")),
        name="ffn",
    )(*args)


def _lambda_full(lam_ref):
    lq1 = lam_ref[0:1, :]
    lk1 = lam_ref[1:2, :]
    lq2 = lam_ref[2:3, :]
    lk2 = lam_ref[3:4, :]
    return (jnp.exp(jnp.sum(lq1 * lk1, axis=-1, keepdims=True))
            - jnp.exp(jnp.sum(lq2 * lk2, axis=-1, keepdims=True)) + LAMBDA_INIT)


def _alibi_slopes():
    return np.asarray(2.0 ** (-8.0 * np.arange(1, ATT_HEADS + 1) / ATT_HEADS), dtype=np.float32)


def _attn_prompt_kernel(q_ref, k_ref, v_ref, slope_ref, lam_ref, subt_ref, o_ref, vt_ref, bias_ref, biasd_ref, m_ref,
                        l_ref, acc_ref, *, t, unroll):
    h = pl.program_id(1)
    i = pl.program_id(2)
    dh = ATT_HEAD_DIM
    nkb = vt_ref.shape[0]
    ns = t // LANES

    @pl.when(i == 0)
    def _():
        for jb in range(nkb):
            for c in range(ns):
                r0 = jb * t + c * LANES
                vt_ref[jb, :, c * LANES:(c + 1) * LANES] = v_ref[r0:r0 + LANES, :].astype(F32).T.astype(BF16)

    slope = slope_ref[pl.ds(h, 1), :][:, 0:1]
    kk = lax.broadcasted_iota(jnp.int32, (t, t), 0)
    qi = lax.broadcasted_iota(jnp.int32, (t, t), 1)
    b = (kk - qi).astype(F32) * slope
    bias_ref[...] = jnp.concatenate([b, b], axis=1)
    bd = jnp.where(kk <= qi, b, -jnp.inf)
    biasd_ref[...] = jnp.concatenate([bd, bd], axis=1)
    q = q_ref[...]
    lane = lax.broadcasted_iota(jnp.int32, (1, 2 * dh), 1)
    zero = jnp.zeros_like(q)
    qq = jnp.concatenate([jnp.where(lane < dh, q, zero), jnp.where(lane >= dh, q, zero)], axis=0)

    m_ref[...] = jnp.full_like(m_ref, -jnp.inf)
    l_ref[...] = jnp.zeros_like(l_ref)
    acc_ref[...] = jnp.zeros_like(acc_ref)

    def blocks(js, diag_last):
        ss = [_dot_nt(k_ref[pl.ds(pl.multiple_of(j * t, t), t), :], qq) for j in js]
        stats = []
        for n, s in enumerate(ss):
            s = s + (biasd_ref[...] if (diag_last and n == len(js) - 1) else bias_ref[...])
            mx = jnp.max(s, axis=0, keepdims=True)
            p = jnp.exp(s - mx)
            stats.append((mx, jnp.sum(p, axis=0, keepdims=True), p.astype(BF16)))
        pvs = [_dot(vt_ref[j], st[2]) for j, st in zip(js, stats)]
        mts = [st[0] + slope * jnp.full((1, 1), (j - i) * t, jnp.int32).astype(F32) for j, st in zip(js, stats)]
        m_old = m_ref[...]
        m_new = m_old
        for mt in mts:
            m_new = jnp.maximum(m_new, mt)
        alpha = jnp.exp(m_old - m_new)
        l_acc = alpha * l_ref[...]
        acc = alpha * acc_ref[...]
        for mt, st, pv in zip(mts, stats, pvs):
            wgt = jnp.exp(mt - m_new)
            l_acc = l_acc + wgt * st[1]
            acc = acc + wgt * pv
        m_ref[...] = m_new
        l_ref[...] = l_acc
        acc_ref[...] = acc

    ng = i // unroll

    def body(g, carry):
        blocks([g * unroll + u for u in range(unroll)], False)
        return carry

    lax.fori_loop(0, ng, body, 0)
    for rem in range(unroll):

        @pl.when(i - ng * unroll == rem)
        def _(rem=rem):
            blocks([ng * unroll + u for u in range(rem)] + [i], True)

    lam = _lambda_full(lam_ref)
    o = acc_ref[...] / l_ref[...]
    o = o[:, 0:t] - lam * o[:, t:2 * t]
    var = jnp.mean(o * o, axis=0, keepdims=True)
    o = o * lax.rsqrt(var + EPS) * subt_ref[:, 0:1] * (1.0 - LAMBDA_INIT)
    for c in range(ns):
        o_ref[c * LANES:(c + 1) * LANES, :] = o[:, c * LANES:(c + 1) * LANES].T.astype(o_ref.dtype)


def attn_prompt(qkv, nseq, seq, t, lam4, subln):
    nq = seq // t
    hw = 2 * ATT_HEAD_DIM
    qkv3 = qkv.reshape(nseq, seq, qkv.shape[1])
    slopes = jnp.asarray(np.tile(_alibi_slopes()[:, None], (1, LANES)))
    subt = jnp.tile(subln.reshape(hw, 1), (1, LANES))
    return pl.pallas_call(
        functools.partial(_attn_prompt_kernel, t=t, unroll=4),
        grid=(nseq, ATT_HEADS, nq),
        in_specs=[pl.BlockSpec((None, t, hw), lambda b, h, i: (b, i, h)),
                  pl.BlockSpec((None, seq, hw), lambda b, h, i: (b, 0, ATT_HEADS + h)),
                  pl.BlockSpec((None, seq, hw), lambda b, h, i: (b, 0, 2 * ATT_HEADS + h)),
                  _resident(slopes.shape), _resident(lam4.shape), _resident(subt.shape)],
        out_specs=pl.BlockSpec((None, t, hw), lambda b, h, i: (b, i, h)),
        out_shape=jax.ShapeDtypeStruct((nseq, seq, ATT_HEADS * ATT_V_DIM), BF16),
        scratch_shapes=[pltpu.VMEM((nq, hw, t), BF16), pltpu.VMEM((t, 2 * t), F32), pltpu.VMEM((t, 2 * t), F32),
                        pltpu.VMEM((1, 2 * t), F32), pltpu.VMEM((1, 2 * t), F32), pltpu.VMEM((hw, 2 * t), F32)],
        compiler_params=_cparams(("parallel", "parallel", "arbitrary")),
        name="attn_prompt",
    )(qkv3, qkv3, qkv3, slopes, lam4, subt).reshape(nseq * seq, ATT_HEADS * ATT_V_DIM)


def _attn_sample_kernel(pt_ref, *refs, npp, past):
    k_refs = refs[:npp]
    v_refs = refs[npp:2 * npp]
    (q_ref, knew_ref, vnew_ref, slope_ref, lam_ref, sub_ref, o_ref, b0_ref, m_ref, l_ref, acc_ref) = refs[2 * npp:]
    j = pl.program_id(1)
    nj = pl.num_programs(1)
    nr = q_ref.shape[0]
    ncol = PAGE_SIZE * ATT_HEADS
    rowi = lax.broadcasted_iota(jnp.int32, (nr, 1), 0)
    slope = slope_ref[:, 0:1]
    tq = (rowi % TOK0).astype(F32)

    @pl.when(j == 0)
    def _():
        m_ref[...] = jnp.full_like(m_ref, -jnp.inf)
        l_ref[...] = jnp.zeros_like(l_ref)
        acc_ref[...] = jnp.zeros_like(acc_ref)
        col = lax.broadcasted_iota(jnp.int32, (1, ncol), 1)
        same_head = (col % ATT_HEADS) == (rowi // GROUP)
        b0_ref[...] = jnp.where(same_head, slope * ((col // ATT_HEADS).astype(F32) - tq), -jnp.inf)

    q = q_ref[...]
    ss = [_dot_nt(q, k_refs[r][...].astype(BF16)) for r in range(npp)]
    stats = []
    for s in ss:
        s = s + b0_ref[...]
        mr = jnp.max(s, axis=-1, keepdims=True)
        p = jnp.exp(s - mr)
        stats.append((mr, jnp.sum(p, axis=-1, keepdims=True), p.astype(BF16)))
    parts = []
    for r, (mr, lr, p) in enumerate(stats):
        pv = _dot(p, v_refs[r][...].astype(BF16))
        page0 = jnp.full((1, 1), (j * npp + r) * PAGE_SIZE, jnp.int32).astype(F32)
        parts.append((mr + slope * (page0 - past), lr, pv))
    m_old = m_ref[...]
    m_new = m_old
    for mt, _, _ in parts:
        m_new = jnp.maximum(m_new, mt)
    alpha = jnp.exp(m_old - m_new)
    l_acc = alpha * l_ref[...]
    acc = alpha * acc_ref[...]
    for mt, lr, pv in parts:
        wgt = jnp.exp(mt - m_new)
        l_acc = l_acc + wgt * lr
        acc = acc + wgt * pv
    m_ref[...] = m_new
    l_ref[...] = l_acc
    acc_ref[...] = acc

    @pl.when(j == nj - 1)
    def _():
        nn = knew_ref.shape[0]
        zpad = jnp.zeros((LANES - nn, LANES), F32)
        kb = jnp.concatenate([knew_ref[...], zpad], axis=0).astype(BF16)
        vb = jnp.concatenate([vnew_ref[...], zpad], axis=0).astype(BF16)
        col = lax.broadcasted_iota(jnp.int32, (1, LANES), 1)
        grow = col // ATT_HEADS
        tk = (grow - TOK0).astype(F32)
        ok = jnp.logical_and(jnp.logical_and(grow >= TOK0, grow < GROUP), tk <= tq)
        ok = jnp.logical_and(ok, (col % ATT_HEADS) == (rowi // GROUP))
        s = jnp.where(ok, _dot_nt(q, kb) - slope * (tq - tk), -jnp.inf)
        m_o = m_ref[...]
        m_n = jnp.maximum(m_o, jnp.max(s, axis=-1, keepdims=True))
        p = jnp.exp(s - m_n)
        al = jnp.exp(m_o - m_n)
        l_fin = al * l_ref[...] + jnp.sum(p, axis=-1, keepdims=True)
        acc_fin = (al * acc_ref[...] + _dot(p.astype(BF16), vb)) / l_fin

        lam = _lambda_full(lam_ref)
        rows8 = lax.broadcasted_iota(jnp.int32, (GROUP, 1), 0)
        for h in range(ATT_HEADS):
            blk = acc_fin[h * GROUP:(h + 1) * GROUP, :]
            o = blk - lam * pltpu.roll(blk, TOK0, axis=0)
            o = _rms(o, sub_ref[...]) * (1.0 - LAMBDA_INIT)
            o_ref[:, h * ATT_V_DIM:(h + 1) * ATT_V_DIM] = jnp.where(rows8 >= TOK0, o, 0.0).astype(o_ref.dtype)


def attn_sample(page_table, cache_k, cache_v, qh, k_new, v_new, lam4, subln, npp):
    nb, n_pages = page_table.shape
    n_pool = cache_k.shape[0]
    dk = ATT_HEADS * ATT_V_DIM
    ncol = PAGE_SIZE * ATT_HEADS
    ck = cache_k.reshape(n_pool, ncol, 2 * ATT_HEAD_DIM)
    cv = cache_v.reshape(n_pool, ncol, ATT_V_DIM)
    nr = qh.shape[1]
    nn = GROUP * ATT_HEADS
    slopes = jnp.asarray(np.tile(np.repeat(_alibi_slopes(), GROUP)[:, None], (1, LANES)))

    def page_spec(r):
        return pl.BlockSpec((None, ncol, LANES), lambda b, j, pt: (pt[b, j * npp + r], 0, 0))

    in_specs = [page_spec(r) for r in range(npp)] + [page_spec(r) for r in range(npp)]
    in_specs += [pl.BlockSpec((None, nr, LANES), lambda b, j, pt: (b, 0, 0)),
                 pl.BlockSpec((nn, LANES), lambda b, j, pt: (b, 0)),
                 pl.BlockSpec((nn, LANES), lambda b, j, pt: (b, 0)),
                 pl.BlockSpec(slopes.shape, lambda b, j, pt: (0, 0)),
                 pl.BlockSpec(lam4.shape, lambda b, j, pt: (0, 0)),
                 pl.BlockSpec(subln.shape, lambda b, j, pt: (0, 0))]
    grid_spec = pltpu.PrefetchScalarGridSpec(
        num_scalar_prefetch=1,
        grid=(nb, n_pages // npp),
        in_specs=in_specs,
        out_specs=pl.BlockSpec((GROUP, dk), lambda b, j, pt: (b, 0)),
        scratch_shapes=[pltpu.VMEM((nr, ncol), F32), pltpu.VMEM((nr, 1), F32), pltpu.VMEM((nr, 1), F32),
                        pltpu.VMEM((nr, LANES), F32)],
    )
    return pl.pallas_call(
        functools.partial(_attn_sample_kernel, npp=npp, past=float(n_pages * PAGE_SIZE)),
        grid_spec=grid_spec,
        out_shape=jax.ShapeDtypeStruct((nb * GROUP, dk), BF16),
        compiler_params=_cparams(("parallel", "arbitrary")),
        name="attn_sample",
    )(page_table, *([ck] * npp), *([cv] * npp), qh, k_new.reshape(nb * nn, LANES), v_new.reshape(nb * nn, LANES),
      slopes, lam4, subln)


def _block_diag4(w):
    nb = w.shape[0] // 4
    eye = jnp.eye(4, dtype=w.dtype)
    w4 = w.reshape(nb, 4, RNN_BW, RNN_BW)
    return jnp.einsum('cajk,ab->cajbk', w4, eye).reshape(nb, 4 * RNN_BW, 4 * RNN_BW).astype(BF16)


def _prep_weights(norm_mix, norm_ffn, norm_final, w_in0, conv_rnn_w, conv_rnn_b, rg_w_a, rg_b_a, rg_w_x, rg_b_x,
                  rg_lambda, conv_ssm_w, conv_ssm_b, dt_bias, a_log, d_skip, ssm_norm, w_out0, w_in1, lambda_q1,
                  lambda_k1, lambda_q2, lambda_k2, subln, w_out1, w_up, ffn_conv_w, ffn_conv_b, w_down):
    row = lambda v: v.reshape(1, -1)
    pad_l = lambda v: jnp.pad(v.reshape(1, -1), ((0, 0), (0, LANES - v.shape[-1])))
    return dict(
        norm_mix=[row(norm_mix[l]) for l in range(2)],
        norm_ffn=[row(norm_ffn[l]) for l in range(2)],
        norm_final=row(norm_final),
        w_in0=jnp.pad(w_in0, ((0, 0), (0, D_IN0_PAD - D_IN0))).astype(BF16),
        conv_rnn_w=conv_rnn_w, conv_rnn_b=row(conv_rnn_b),
        wa=_block_diag4(rg_w_a), ba=row(rg_b_a), wx=_block_diag4(rg_w_x), bx=row(rg_b_x), lam=row(rg_lambda),
        conv_ssm_w=conv_ssm_w, conv_ssm_b=row(conv_ssm_b),
        dt_bias=pad_l(dt_bias), a_log=pad_l(a_log),
        d_skip=row(jnp.repeat(d_skip, SSM_HEAD_DIM)), ssm_norm=row(ssm_norm),
        w_out0a=w_out0[:D_RNN].astype(BF16), w_out0b=w_out0[D_RNN:].astype(BF16),
        w_in1=w_in1.astype(BF16),
        lam4=jnp.stack([lambda_q1, lambda_k1, lambda_q2, lambda_k2]), subln=row(subln),
        w_out1=w_out1.astype(BF16),
        w_up=[w_up[l].astype(BF16) for l in range(2)],
        ffn_conv_w=[ffn_conv_w[l] for l in range(2)], ffn_conv_b=[row(ffn_conv_b[l]) for l in range(2)],
        w_down=[w_down[l].astype(BF16) for l in range(2)],
    )


def _state_t(s):
    b = s.shape[0]
    return jnp.transpose(s, (0, 3, 1, 2)).reshape(b, SSM_STATE, D_SSM)


def _state_untranspose(st):
    b = st.shape[0]
    return jnp.transpose(st.reshape(b, SSM_STATE, SSM_HEADS, SSM_HEAD_DIM), (0, 2, 3, 1))


def _prompt_trunk(x_prompt, w):
    bsz, seq, d = x_prompt.shape
    m = bsz * seq
    x = x_prompt.reshape(m, d)
    tm = min(512, seq)
    tr = min(256, seq)

    p0 = proj0(x, w['norm_mix'][0], w['w_in0'], tm)
    rnn_out, h_tail = rglru(p0, bsz, seq, tr, w['conv_rnn_w'], w['conv_rnn_b'], w['wa'], w['ba'], w['wx'],
                            w['bx'], w['lam'])
    lc = SSD_CHUNK
    nch = seq // lc
    hb = lc // HALO
    xbc_cb = (2 * D_RNN + D_SSM) // SSM_CONV_DIM
    halo_spec = pl.BlockSpec(
        (HALO, SSM_CONV_DIM), lambda b, i: (jnp.maximum((b * nch + i) * hb - 1, 0), xbc_cb))
    y_ssm, st = ssd(p0, xbc_cb, p0, halo_spec, p0, 2 * D_RNN // D_SSM, p0, (D_IN0_PAD - LANES) // LANES,
                    bsz, nch, lc, lc, w['conv_ssm_w'], w['conv_ssm_b'], w['dt_bias'], w['a_log'],
                    w['d_skip'], w['ssm_norm'])
    x = out_proj([rnn_out, y_ssm], [w['w_out0a'], w['w_out0b']], x, tm)
    x, u0 = ffn(x, bsz, seq, tm, w['norm_ffn'][0], w['w_up'][0], w['ffn_conv_w'][0], w['ffn_conv_b'][0],
                w['w_down'][0], w['norm_final'], False)

    k, v, qkv = proj1(x, w['norm_mix'][1], w['w_in1'], tm)
    o = attn_prompt(qkv, bsz, seq, min(256, seq), w['lam4'], w['subln'])
    x = out_proj([o], [w['w_out1']], x, tm)
    y, u1 = ffn(x, bsz, seq, tm, w['norm_ffn'][1], w['w_up'][1], w['ffn_conv_w'][1], w['ffn_conv_b'][1],
                w['w_down'][1], w['norm_final'], True)

    p3 = p0.reshape(bsz, seq, D_IN0_PAD)
    rnn_conv = p3[:, seq - (CONV_W - 1):, 0:D_RNN]
    ssm_conv = p3[:, seq - (CONV_W - 1):, 2 * D_RNN + D_SSM:2 * D_RNN + D_SSM + SSM_CONV_DIM]
    ffn_conv = jnp.stack([u0[:, SUBLANES - (FFN_CONV_W - 1):], u1[:, SUBLANES - (FFN_CONV_W - 1):]])
    return (y.reshape(bsz, seq, d), rnn_conv, h_tail[:, SUBLANES - 1], ssm_conv, _state_untranspose(st),
            k.reshape(bsz, seq, ATT_HEADS, 2 * ATT_HEAD_DIM), v.reshape(bsz, seq, ATT_HEADS, ATT_V_DIM), ffn_conv)


def _to_groups(x_tok, hist=None):
    b, t, c = x_tok.shape
    if hist is None:
        lead = jnp.zeros((b, TOK0, c), x_tok.dtype)
    else:
        k = hist.shape[1]
        lead = jnp.concatenate([jnp.zeros((b, TOK0 - k, c), x_tok.dtype), hist], axis=1)
    return jnp.concatenate([lead, x_tok], axis=1).reshape(b * GROUP, c)


def _sample_trunk(x_sample, state_rnn_conv, state_rnn_h, state_ssm_conv, state_ssm, cache_k, cache_v,
                  state_ffn_conv, page_table, w):
    nb, t, d = x_sample.shape
    assert t == GROUP - TOK0
    m = nb * GROUP
    x = _to_groups(x_sample)

    p0 = proj0(x, w['norm_mix'][0], w['w_in0'], m)
    p0g = p0.reshape(nb, GROUP, D_IN0_PAD)
    c_xbc = 2 * D_RNN + D_SSM
    xr = _to_groups(p0g[:, TOK0:, 0:D_RNN], state_rnn_conv)
    p0r = jnp.concatenate([xr, p0[:, D_RNN:2 * D_RNN]], axis=1)
    hinj = _to_groups(jnp.zeros((nb, t, D_RNN), F32), state_rnn_h[:, None, :])
    rnn_out, h_all = rglru(p0r, 1, m, m, w['conv_rnn_w'], w['conv_rnn_b'], w['wa'], w['ba'], w['wx'], w['bx'],
                           w['lam'], hinj=hinj)
    lc = SSD_CHUNK
    padc = lambda a: jnp.pad(a, ((0, 0), (0, lc - t), (0, 0))).reshape(nb * lc, a.shape[-1])
    xbc_p = padc(p0g[:, TOK0:, c_xbc:c_xbc + SSM_CONV_DIM])
    z_p = padc(p0g[:, TOK0:, 2 * D_RNN:2 * D_RNN + D_SSM])
    dt_p = padc(p0g[:, TOK0:, D_IN0_PAD - LANES:])
    halo = jnp.concatenate([jnp.zeros((nb, HALO - (CONV_W - 1), SSM_CONV_DIM), F32), state_ssm_conv], axis=1)
    halo_spec = pl.BlockSpec((None, HALO, SSM_CONV_DIM), lambda b, i: (b, 0, 0))
    y_p, st = ssd(xbc_p, 0, halo, halo_spec, z_p, 0, dt_p, 0, nb, 1, lc, t, w['conv_ssm_w'], w['conv_ssm_b'],
                  w['dt_bias'], w['a_log'], w['d_skip'], w['ssm_norm'], s0=_state_t(state_ssm))
    y_ssm = _to_groups(y_p.reshape(nb, lc, D_SSM)[:, 0:t])
    x = out_proj([rnn_out, y_ssm], [w['w_out0a'], w['w_out0b']], x, m)
    uinj0 = _to_groups(jnp.zeros((nb, t, 2 * D_FF), F32), state_ffn_conv[0])
    x, u0 = ffn(x, 1, m, m, w['norm_ffn'][0], w['w_up'][0], w['ffn_conv_w'][0], w['ffn_conv_b'][0],
                w['w_down'][0], w['norm_final'], False, uinj=uinj0)

    k, v, qkv = proj1(x, w['norm_mix'][1], w['w_in1'], m)
    qg = qkv[:, 0:D_MODEL].reshape(nb, GROUP, ATT_HEADS, 2, ATT_HEAD_DIM)[:, TOK0:]
    flip = jnp.array([[0, 1], [1, 0]], dtype=BF16)
    qh = jnp.einsum('bqhmd,nm->bhnqmd', qg, flip).reshape(nb, ATT_HEADS * GROUP, 2 * ATT_HEAD_DIM)
    o = attn_sample(page_table, cache_k, cache_v, qh, k, v, w['lam4'], w['subln'], npp=8)
    x = out_proj([o], [w['w_out1']], x, m)
    uinj1 = _to_groups(jnp.zeros((nb, t, 2 * D_FF), F32), state_ffn_conv[1])
    y, u1 = ffn(x, 1, m, m, w['norm_ffn'][1], w['w_up'][1], w['ffn_conv_w'][1], w['ffn_conv_b'][1],
                w['w_down'][1], w['norm_final'], True, uinj=uinj1)

    tok = lambda a: a.reshape(nb, GROUP, a.shape[-1])[:, TOK0:]
    rnn_conv = p0g[:, GROUP - (CONV_W - 1):, 0:D_RNN]
    ssm_conv = p0g[:, GROUP - (CONV_W - 1):, c_xbc:c_xbc + SSM_CONV_DIM]
    ffn_conv = jnp.stack([u0.reshape(nb, GROUP, -1)[:, GROUP - (FFN_CONV_W - 1):],
                          u1.reshape(nb, GROUP, -1)[:, GROUP - (FFN_CONV_W - 1):]])
    return (tok(y), rnn_conv, h_all.reshape(nb, GROUP, D_RNN)[:, GROUP - 1], ssm_conv, _state_untranspose(st),
            tok(k).reshape(nb, t, ATT_HEADS, 2 * ATT_HEAD_DIM), tok(v).reshape(nb, t, ATT_HEADS, ATT_V_DIM),
            ffn_conv)


def kernel(x_prompt, x_sample, state_rnn_conv, state_rnn_h, state_ssm_conv, state_ssm, cache_k, cache_v, state_ffn_conv, page_table, norm_mix, norm_ffn, norm_final, w_in0, conv_rnn_w, conv_rnn_b, rg_w_a, rg_b_a, rg_w_x, rg_b_x, rg_lambda, conv_ssm_w, conv_ssm_b, dt_bias, a_log, d_skip, ssm_norm, w_out0, w_in1, lambda_q1, lambda_k1, lambda_q2, lambda_k2, subln, w_out1, w_up, ffn_conv_w, ffn_conv_b, w_down):
    w = _prep_weights(norm_mix, norm_ffn, norm_final, w_in0, conv_rnn_w, conv_rnn_b, rg_w_a, rg_b_a, rg_w_x,
                      rg_b_x, rg_lambda, conv_ssm_w, conv_ssm_b, dt_bias, a_log, d_skip, ssm_norm, w_out0, w_in1,
                      lambda_q1, lambda_k1, lambda_q2, lambda_k2, subln, w_out1, w_up, ffn_conv_w, ffn_conv_b,
                      w_down)
    p = _prompt_trunk(x_prompt, w)
    s = _sample_trunk(x_sample, state_rnn_conv, state_rnn_h, state_ssm_conv, state_ssm, cache_k, cache_v,
                      state_ffn_conv, page_table, w)
    return (p[0], s[0]) + p[1:] + s[1:]
```

```python
import functools
import math

import jax
import jax.numpy as jnp
import numpy as np
from jax import lax
from jax.experimental import pallas as pl
from jax.experimental.pallas import tpu as pltpu

F32 = jnp.float32
BF16 = jnp.bfloat16

D_MODEL = 1024
D_RNN = 1024
RNN_BW = 64
RG_C = 8.0
CONV_W = 4
SSM_HEADS = 16
SSM_HEAD_DIM = 64
D_SSM = SSM_HEADS * SSM_HEAD_DIM
SSM_GROUPS = 2
SSM_STATE = 128
SSM_CONV_DIM = D_SSM + 2 * SSM_GROUPS * SSM_STATE
SSD_CHUNK = 128
D_IN0 = 2 * D_RNN + D_SSM + SSM_CONV_DIM + SSM_HEADS
D_IN0_PAD = 2 * D_RNN + D_SSM + SSM_CONV_DIM + 128
ATT_HEADS = 8
ATT_HEAD_DIM = 64
ATT_V_DIM = 128
LAMBDA_INIT = 0.8 - 0.6 * math.exp(-0.3 * 1)
D_FF = 3 * D_MODEL
FFN_CONV_W = 3
EPS = 1e-6
LOG2E = math.log2(math.e)
PAGE_SIZE = 128

SUBLANES = 8
BF16_ROWS = 16
LANES = 128
HALO = SUBLANES
GROUP = 8
TOK0 = 4
VMEM_LIMIT = 56 * 1024 * 1024


def _cparams(sem):
    return pltpu.CompilerParams(dimension_semantics=sem, vmem_limit_bytes=VMEM_LIMIT)


def _resident(shape):
    nd = len(shape)
    return pl.BlockSpec(shape, lambda *_: (0,) * nd, pipeline_mode=pl.Buffered(1))


def _rms(x, g):
    var = jnp.mean(x * x, axis=-1, keepdims=True)
    return x * lax.rsqrt(var + EPS) * g


def _softplus(x):
    return jnp.maximum(x, 0.0) + jnp.log1p(jnp.exp(-jnp.abs(x)))


def _dot(a, b):
    return jnp.dot(a, b, preferred_element_type=F32)


def _dot_nt(a, b):
    return lax.dot_general(a, b, (((1,), (1,)), ((), ())), preferred_element_type=F32)


def _proj0_kernel(x_ref, g_ref, w_ref, o_ref, *, tn):
    xn = _rms(x_ref[...], g_ref[...]).astype(BF16)
    n = w_ref.shape[1]
    for c in range(0, n, tn):
        w = min(tn, n - c)
        o_ref[:, c:c + w] = _dot(xn, w_ref[:, c:c + w])


def proj0(x, g, w, tm):
    m, d = x.shape
    n = w.shape[1]
    return pl.pallas_call(
        functools.partial(_proj0_kernel, tn=512),
        grid=(m // tm,),
        in_specs=[pl.BlockSpec((tm, d), lambda i: (i, 0)), _resident((1, d)), _resident((d, n))],
        out_specs=pl.BlockSpec((tm, n), lambda i: (i, 0)),
        out_shape=jax.ShapeDtypeStruct((m, n), F32),
        compiler_params=_cparams(("parallel",)),
        name="proj0",
    )(x, g, w)


def _proj1_kernel(x_ref, g_ref, w_ref, k_ref, v_ref, qkv_ref, *, tn):
    xn = _rms(x_ref[...], g_ref[...]).astype(BF16)
    d = k_ref.shape[1]
    for c in range(0, 3 * d, tn):
        r = _dot(xn, w_ref[:, c:c + tn])
        if c < d:
            qkv_ref[:, c:c + tn] = (r * (ATT_HEAD_DIM ** -0.5 * LOG2E)).astype(BF16)
        else:
            qkv_ref[:, c:c + tn] = r.astype(BF16)
            if c < 2 * d:
                k_ref[:, c - d:c - d + tn] = r
            else:
                v_ref[:, c - 2 * d:c - 2 * d + tn] = r


def proj1(x, g, w, tm):
    m, d = x.shape
    n = w.shape[1]
    return pl.pallas_call(
        functools.partial(_proj1_kernel, tn=512),
        grid=(m // tm,),
        in_specs=[pl.BlockSpec((tm, d), lambda i: (i, 0)), _resident((1, d)), _resident((d, n))],
        out_specs=[pl.BlockSpec((tm, d), lambda i: (i, 0)), pl.BlockSpec((tm, d), lambda i: (i, 0)),
                   pl.BlockSpec((tm, n), lambda i: (i, 0))],
        out_shape=[jax.ShapeDtypeStruct((m, d), F32), jax.ShapeDtypeStruct((m, d), F32),
                   jax.ShapeDtypeStruct((m, n), BF16)],
        compiler_params=_cparams(("parallel",)),
        name="proj1",
    )(x, g, w)


def _out_proj_kernel(*refs, n_in, tn):
    a_refs = refs[:n_in]
    w_refs = refs[n_in:2 * n_in]
    res_ref = refs[2 * n_in]
    o_ref = refs[2 * n_in + 1]
    n = o_ref.shape[1]
    for c in range(0, n, tn):
        acc = res_ref[:, c:c + tn]
        for a_ref, w_ref in zip(a_refs, w_refs):
            acc = acc + _dot(a_ref[...], w_ref[:, c:c + tn])
        o_ref[:, c:c + tn] = acc


def out_proj(a_list, w_list, res, tm):
    m, n = res.shape
    n_in = len(a_list)
    in_specs = [pl.BlockSpec((tm, a.shape[1]), lambda i: (i, 0)) for a in a_list]
    in_specs += [_resident(w.shape) for w in w_list]
    in_specs += [pl.BlockSpec((tm, n), lambda i: (i, 0))]
    return pl.pallas_call(
        functools.partial(_out_proj_kernel, n_in=n_in, tn=512),
        grid=(m // tm,),
        in_specs=in_specs,
        out_specs=pl.BlockSpec((tm, n), lambda i: (i, 0)),
        out_shape=jax.ShapeDtypeStruct((m, n), F32),
        compiler_params=_cparams(("parallel",)),
        name="out_proj",
    )(*a_list, *w_list, res)


def _conv_from_ext(ext_ref, w_ref, b_ref, width, tm, c0, c1):
    y = b_ref[:, c0:c1]
    for j in range(width):
        off = HALO - (width - 1) + j
        y = y + ext_ref[off:off + tm, c0:c1] * w_ref[j:j + 1, c0:c1]
    return y


def _scan_rows(a, b, c):
    tm = a.shape[0]
    rowm = lax.broadcasted_iota(jnp.int32, a.shape, 0) % SUBLANES
    s = 1
    while s < SUBLANES:
        keep = rowm >= s
        a_s = jnp.where(keep, pltpu.roll(a, s, axis=0), 1.0)
        b_s = jnp.where(keep, pltpu.roll(b, s, axis=0), 0.0)
        b = a * b_s + b
        a = a * a_s
        s *= 2
    hs = []
    for g in range(tm // SUBLANES):
        hg = a[g * SUBLANES:(g + 1) * SUBLANES] * c + b[g * SUBLANES:(g + 1) * SUBLANES]
        c = hg[SUBLANES - 1:SUBLANES, :]
        hs.append(hg)
    return jnp.concatenate(hs, axis=0)


def _rglru_kernel(*refs, tm, sample):
    if sample:
        (x_ref, y_ref, cw_ref, cb_ref, wa_ref, ba_ref, wx_ref, bx_ref, lam_ref, hinj_ref,
         out_ref, h_out_ref, ext_ref, carry_ref) = refs
    else:
        (x_ref, y_ref, halo_ref, cw_ref, cb_ref, wa_ref, ba_ref, wx_ref, bx_ref, lam_ref,
         out_ref, h_out_ref, ext_ref, carry_ref) = refs
    i = pl.program_id(1)
    d = x_ref.shape[1]

    @pl.when(i == 0)
    def _():
        carry_ref[...] = jnp.zeros_like(carry_ref)

    if sample:
        ext_ref[0:HALO, :] = jnp.zeros((HALO, d), F32)
    else:
        ext_ref[0:HALO, :] = jnp.where(i == 0, 0.0, halo_ref[...])
    ext_ref[HALO:HALO + tm, :] = x_ref[...]

    row = lax.broadcasted_iota(jnp.int32, (tm, 1), 0)
    tail = h_out_ref.shape[0]
    sp = _softplus(-lam_ref[...])
    cbw = 4 * RNN_BW
    for c in range(d // cbw):
        c0, c1 = c * cbw, (c + 1) * cbw
        xr = _conv_from_ext(ext_ref, cw_ref, cb_ref, CONV_W, tm, c0, c1)
        xb = xr.astype(BF16)
        r = jax.nn.sigmoid(_dot(xb, wa_ref[c]) + ba_ref[:, c0:c1])
        ig = jax.nn.sigmoid(_dot(xb, wx_ref[c]) + bx_ref[:, c0:c1])
        log_a = -RG_C * r * sp[:, c0:c1]
        a = jnp.exp(log_a)
        mult = jnp.sqrt(jnp.maximum(1.0 - a * a, 0.0))
        if sample:
            inj = (row % GROUP) == (TOK0 - 1)
            a = jnp.where(inj, 0.0, a)
            bv = jnp.where(inj, hinj_ref[:, c0:c1], xr * ig * mult)
        else:
            first = jnp.logical_and(i == 0, row == 0)
            a = jnp.where(first, 0.0, a)
            mult = jnp.where(first, 1.0, mult)
            bv = xr * ig * mult
        for k in range(cbw // LANES):
            l0, l1 = c0 + k * LANES, c0 + (k + 1) * LANES
            h = _scan_rows(a[:, k * LANES:(k + 1) * LANES], bv[:, k * LANES:(k + 1) * LANES], carry_ref[:, l0:l1])
            carry_ref[:, l0:l1] = h[tm - 1:tm, :]
            out_ref[:, l0:l1] = (h * jax.nn.gelu(y_ref[:, l0:l1])).astype(out_ref.dtype)
            h_out_ref[:, l0:l1] = h[tm - tail:tm, :]


def rglru(proj, nseq, seq, tm, cw, cb, wa, ba, wx, bx, lam, hinj=None):
    sample = hinj is not None
    nt = seq // tm
    d = D_RNN
    xmap = lambda b, i: (b * nt + i, 0)
    in_specs = [pl.BlockSpec((tm, d), xmap), pl.BlockSpec((tm, d), lambda b, i: (b * nt + i, 1))]
    args = [proj, proj]
    if not sample:
        hb = tm // HALO
        in_specs.append(pl.BlockSpec((HALO, d), lambda b, i: (jnp.maximum((b * nt + i) * hb - 1, 0), 0)))
        args.append(proj)
    in_specs += [_resident(cw.shape), _resident(cb.shape), _resident(wa.shape), _resident(ba.shape),
                 _resident(wx.shape), _resident(bx.shape), _resident(lam.shape)]
    args += [cw, cb, wa, ba, wx, bx, lam]
    if sample:
        in_specs.append(pl.BlockSpec((tm, d), xmap))
        args.append(hinj)
        h_spec = pl.BlockSpec((tm, d), xmap)
        h_shape = jax.ShapeDtypeStruct((nseq * seq, d), F32)
    else:
        h_spec = pl.BlockSpec((None, SUBLANES, d), lambda b, i: (b, 0, 0))
        h_shape = jax.ShapeDtypeStruct((nseq, SUBLANES, d), F32)
    return pl.pallas_call(
        functools.partial(_rglru_kernel, tm=tm, sample=sample),
        grid=(nseq, nt),
        in_specs=in_specs,
        out_specs=[pl.BlockSpec((tm, d), xmap), h_spec],
        out_shape=[jax.ShapeDtypeStruct((nseq * seq, d), BF16), h_shape],
        scratch_shapes=[pltpu.VMEM((tm + HALO, d), F32), pltpu.VMEM((1, d), F32)],
        compiler_params=_cparams(("parallel", "arbitrary")),
        name="rglru",
    )(*args)


def _pair(v, hp, lane_lo):
    return jnp.where(lane_lo, v[:, 2 * hp:2 * hp + 1], v[:, 2 * hp + 1:2 * hp + 2])


def _ssd_kernel(*refs, lc, n_valid, has_init):
    if has_init:
        (xbc_ref, halo_ref, z_ref, dt_ref, cw_ref, cb_ref, dtb_ref, alog_ref, dsk_ref, nrm_ref, s0_ref,
         y_ref, sout_ref, ext_ref, st_ref, yacc_ref) = refs
    else:
        (xbc_ref, halo_ref, z_ref, dt_ref, cw_ref, cb_ref, dtb_ref, alog_ref, dsk_ref, nrm_ref,
         y_ref, sout_ref, ext_ref, st_ref, yacc_ref) = refs
    i = pl.program_id(1)
    n = SSM_STATE
    hd2 = 2 * SSM_HEAD_DIM
    gw = D_SSM // SSM_GROUPS

    @pl.when(i == 0)
    def _():
        if has_init:
            st_ref[...] = s0_ref[...]
        else:
            st_ref[...] = jnp.zeros_like(st_ref)

    if has_init:
        ext_ref[0:HALO, :] = halo_ref[...]
    else:
        ext_ref[0:HALO, :] = jnp.where(i == 0, 0.0, halo_ref[...])
    ext_ref[HALO:HALO + lc, :] = xbc_ref[...]
    xbc = _conv_from_ext(ext_ref, cw_ref, cb_ref, CONV_W, lc, 0, SSM_CONV_DIM)
    xbc = xbc * jax.nn.sigmoid(xbc)
    xs = xbc[:, 0:D_SSM]

    row = lax.broadcasted_iota(jnp.int32, (lc, 1), 0)
    dt = _softplus(dt_ref[...] + dtb_ref[...])
    if n_valid < lc:
        dt = jnp.where(row < n_valid, dt, 0.0)
    acs = dt * (-jnp.exp(alog_ref[...]))
    s = 1
    while s < lc:
        acs = acs + jnp.where(row >= s, pltpu.roll(acs, s, axis=0), 0.0)
        s *= 2
    if lc == LANES:
        acs_t = acs.T
    else:
        acs_t = jnp.concatenate([acs, jnp.zeros((LANES - lc, LANES), F32)], axis=0).T[:, 0:lc]
    a_last = acs[lc - 1:lc, :]
    ea = jnp.exp(acs)
    dsv = jnp.exp(a_last - acs)
    cdec = jnp.exp(a_last)

    lane_lo = lax.broadcasted_iota(jnp.int32, (1, hd2), 1) < SSM_HEAD_DIM
    causal = (lax.broadcasted_iota(jnp.int32, (lc, lc), 0) >= lax.broadcasted_iota(jnp.int32, (lc, lc), 1))
    for g in range(SSM_GROUPS):
        bm = xbc[:, D_SSM + g * n:D_SSM + (g + 1) * n]
        cm = xbc[:, D_SSM + SSM_GROUPS * n + g * n:D_SSM + SSM_GROUPS * n + (g + 1) * n]
        bm_b = bm.astype(BF16)
        cm_b = cm.astype(BF16)
        cbm = _dot_nt(cm_b, bm_b)
        st_old = st_ref[:, g * gw:(g + 1) * gw]
        yoff = _dot(cm_b, st_old.astype(BF16))
        if lc == LANES:
            bm_t = bm.T.astype(BF16)
        else:
            bm_t = jnp.concatenate([bm, jnp.zeros((LANES - lc, n), F32)], axis=0).T[:, 0:lc].astype(BF16)
        w_parts = []
        cd_parts = []
        for pp in range(gw // hd2):
            hp = g * (gw // hd2) + pp
            h0, h1 = 2 * hp, 2 * hp + 1
            c0 = hp * hd2
            xg = xs[:, c0:c0 + hd2] * _pair(dt, hp, lane_lo)
            xg_b = xg.astype(BF16)
            yd = []
            for h in (h0, h1):
                seg = acs[:, h:h + 1] - acs_t[h:h + 1, :]
                dec = jnp.exp(jnp.where(causal, seg, -jnp.inf))
                yd.append(_dot((cbm * dec).astype(BF16), xg_b))
            ydiag = jnp.where(lane_lo, yd[0], yd[1])
            yo = yoff[:, pp * hd2:(pp + 1) * hd2] * _pair(ea, hp, lane_lo)
            yacc_ref[:, c0:c0 + hd2] = ydiag + yo + dsk_ref[:, c0:c0 + hd2] * xs[:, c0:c0 + hd2]
            w_parts.append((xg * _pair(dsv, hp, lane_lo)).astype(BF16))
            cd_parts.append(_pair(cdec, hp, lane_lo))
        w_all = jnp.concatenate(w_parts, axis=1)
        cd_all = jnp.concatenate(cd_parts, axis=1)
        st_ref[:, g * gw:(g + 1) * gw] = cd_all * st_old + _dot(bm_t, w_all)

    z = z_ref[...]
    y = yacc_ref[...] * (z * jax.nn.sigmoid(z))
    y_ref[...] = _rms(y, nrm_ref[...]).astype(y_ref.dtype)
    sout_ref[...] = st_ref[...]


def ssd(xbc_arr, xbc_col, halo_arr, halo_map, z_arr, z_col, dt_arr, dt_col, nseq, nchunk, lc, n_valid,
        cw, cb, dtb, alog, dsk, nrm, s0=None):
    has_init = s0 is not None
    rmap = lambda col: (lambda b, i: (b * nchunk + i, col))
    in_specs = [pl.BlockSpec((lc, SSM_CONV_DIM), rmap(xbc_col)),
                halo_map,
                pl.BlockSpec((lc, D_SSM), rmap(z_col)),
                pl.BlockSpec((lc, LANES), rmap(dt_col)),
                _resident(cw.shape), _resident(cb.shape), _resident(dtb.shape), _resident(alog.shape),
                _resident(dsk.shape), _resident(nrm.shape)]
    args = [xbc_arr, halo_arr, z_arr, dt_arr, cw, cb, dtb, alog, dsk, nrm]
    if has_init:
        in_specs.append(pl.BlockSpec((None, SSM_STATE, D_SSM), lambda b, i: (b, 0, 0)))
        args.append(s0)
    return pl.pallas_call(
        functools.partial(_ssd_kernel, lc=lc, n_valid=n_valid, has_init=has_init),
        grid=(nseq, nchunk),
        in_specs=in_specs,
        out_specs=[pl.BlockSpec((lc, D_SSM), lambda b, i: (b * nchunk + i, 0)),
                   pl.BlockSpec((None, SSM_STATE, D_SSM), lambda b, i: (b, 0, 0))],
        out_shape=[jax.ShapeDtypeStruct((nseq * nchunk * lc, D_SSM), BF16),
                   jax.ShapeDtypeStruct((nseq, SSM_STATE, D_SSM), F32)],
        scratch_shapes=[pltpu.VMEM((lc + HALO, SSM_CONV_DIM), F32), pltpu.VMEM((SSM_STATE, D_SSM), F32),
                        pltpu.VMEM((lc, D_SSM), F32)],
        compiler_params=_cparams(("parallel", "arbitrary")),
        name="ssd",
    )(*args)


def _ffn_kernel(*refs, tm, fc, sample, final_norm):
    if sample:
        (x_ref, g_ref, wup_ref, cw_ref, cb_ref, wdn_ref, gf_ref, uinj_ref,
         o_ref, u_out_ref, ext_ref) = refs
    else:
        (x_ref, halo_ref, g_ref, wup_ref, cw_ref, cb_ref, wdn_ref, gf_ref,
         o_ref, u_out_ref, ext_ref) = refs
    i = pl.program_id(1)
    d_ff = wdn_ref.shape[0]
    x = x_ref[...]
    hn = _rms(x, g_ref[...]).astype(BF16)
    if sample:
        row = lax.broadcasted_iota(jnp.int32, (tm, 1), 0) % GROUP
        hist = jnp.logical_and(row >= TOK0 - (FFN_CONV_W - 1), row < TOK0)
    else:
        halo_n = _rms(halo_ref[...], g_ref[...]).astype(BF16)
    tail = u_out_ref.shape[0]

    def up(c):
        out = []
        for base in (c, d_ff + c):
            u = _dot(hn, wup_ref[:, base:base + fc])
            uh = None if sample else _dot(halo_n, wup_ref[:, base:base + fc])
            out.append((u, uh))
        return out

    acc = x
    ups = up(0)
    for c in range(0, d_ff, fc):
        ups_next = up(c + fc) if c + fc < d_ff else None
        conv = []
        for n, base in enumerate((c, d_ff + c)):
            u, uh = ups[n]
            e_ref = ext_ref.at[n]
            if sample:
                u = jnp.where(hist, uinj_ref[:, base:base + fc], u)
                e_ref[0:HALO, :] = jnp.zeros((HALO, fc), F32)
            else:
                e_ref[0:HALO, :] = jnp.where(i == 0, 0.0, uh)
            e_ref[HALO:HALO + tm, :] = u
            u_out_ref[:, base:base + fc] = e_ref[HALO + tm - tail:HALO + tm, :]
            y = cb_ref[:, base:base + fc]
            for j in range(FFN_CONV_W):
                off = HALO - (FFN_CONV_W - 1) + j
                y = y + e_ref[off:off + tm, :] * cw_ref[j:j + 1, base:base + fc]
            conv.append(y)
        act = (jax.nn.gelu(conv[0]) * conv[1]).astype(BF16)
        acc = acc + _dot(act, wdn_ref[c:c + fc, :])
        ups = ups_next
    if final_norm:
        acc = _rms(acc, gf_ref[...])
    o_ref[...] = acc


def ffn(x, nseq, seq, tm, g, wup, cw, cb, wdn, gf, final_norm, uinj=None):
    sample = uinj is not None
    nt = seq // tm
    d = x.shape[1]
    d_up = wup.shape[1]
    xmap = lambda b, i: (b * nt + i, 0)
    in_specs = [pl.BlockSpec((tm, d), xmap)]
    args = [x]
    if not sample:
        hb = tm // HALO
        in_specs.append(pl.BlockSpec((HALO, d), lambda b, i: (jnp.maximum((b * nt + i) * hb - 1, 0), 0)))
        args.append(x)
    in_specs += [_resident(g.shape), _resident(wup.shape), _resident(cw.shape), _resident(cb.shape),
                 _resident(wdn.shape), _resident(gf.shape)]
    args += [g, wup, cw, cb, wdn, gf]
    if sample:
        in_specs.append(pl.BlockSpec((tm, d_up), xmap))
        args.append(uinj)
        u_spec = pl.BlockSpec((tm, d_up), xmap)
        u_shape = jax.ShapeDtypeStruct((nseq * seq, d_up), F32)
    else:
        u_spec = pl.BlockSpec((None, SUBLANES, d_up), lambda b, i: (b, 0, 0))
        u_shape = jax.ShapeDtypeStruct((nseq, SUBLANES, d_up), F32)
    fc = 512
    return pl.pallas_call(
        functools.partial(_ffn_kernel, tm=tm, fc=fc, sample=sample, final_norm=final_norm),
        grid=(nseq, nt),
        in_specs=in_specs,
        out_specs=[pl.BlockSpec((tm, d), xmap), u_spec],
        out_shape=[jax.ShapeDtypeStruct((nseq * seq, d), F32), u_shape],
        scratch_shapes=[pltpu.VMEM((2, tm + HALO, fc), F32)],
        compiler_params=_cparams(("parallel", "arbitrary")),
        name="ffn",
    )(*args)


def _lambda_full(lam_ref):
    lq1 = lam_ref[0:1, :]
    lk1 = lam_ref[1:2, :]
    lq2 = lam_ref[2:3, :]
    lk2 = lam_ref[3:4, :]
    return (jnp.exp(jnp.sum(lq1 * lk1, axis=-1, keepdims=True))
            - jnp.exp(jnp.sum(lq2 * lk2, axis=-1, keepdims=True)) + LAMBDA_INIT)


def _alibi_slopes():
    return np.asarray(2.0 ** (-8.0 * np.arange(1, ATT_HEADS + 1) / ATT_HEADS), dtype=np.float32)


def _attn_prompt_kernel(q_ref, k_ref, v_ref, slope_ref, lam_ref, subt_ref, o_ref, vt_ref, bias_ref, s_ref, p_ref,
                        m_ref, acc_ref, *, t, unroll):
    h = pl.program_id(1)
    i = pl.program_id(2)
    dh = ATT_HEAD_DIM
    hw = 2 * dh
    nkb = vt_ref.shape[0]
    ns = t // LANES

    @pl.when(i == 0)
    def _():
        for jb in range(nkb):
            for c in range(ns):
                r0 = jb * t + c * LANES
                vt_ref[jb, 0:hw, c * LANES:(c + 1) * LANES] = v_ref[r0:r0 + LANES, :].astype(F32).T.astype(BF16)
            ones_row = lax.broadcasted_iota(jnp.int32, (BF16_ROWS, t), 0) == 0
            vt_ref[jb, hw:hw + BF16_ROWS, :] = jnp.where(ones_row, 1.0, 0.0).astype(BF16)

    slope = slope_ref[pl.ds(h, 1), :][:, 0:1]
    kk = lax.broadcasted_iota(jnp.int32, (t, t), 0)
    qi = lax.broadcasted_iota(jnp.int32, (t, t), 1)
    b = (kk - qi).astype(F32) * (slope * LOG2E)
    bias_ref[...] = jnp.concatenate([b, b], axis=1)
    q = q_ref[...]
    lane = lax.broadcasted_iota(jnp.int32, (1, 2 * dh), 1)
    zero = jnp.zeros_like(q)
    qq = jnp.concatenate([jnp.where(lane < dh, q, zero), jnp.where(lane >= dh, q, zero)], axis=0)

    m_ref[...] = jnp.full_like(m_ref, -jnp.inf)
    acc_ref[...] = jnp.zeros_like(acc_ref)

    def put_scores(j, slot):
        s_ref[slot] = _dot_nt(k_ref[pl.ds(pl.multiple_of(j * t, t), t), :], qq) + bias_ref[...]

    def strip(slot, pslot, c, off, diag):
        cb = c % ns
        nk = (cb + 1) * LANES if diag else t
        sl = slice(c * LANES, (c + 1) * LANES)
        if diag:
            kidx = lax.broadcasted_iota(jnp.int32, (nk, LANES), 0)
            qidx = lax.broadcasted_iota(jnp.int32, (nk, LANES), 1) + cb * LANES
            causal = kidx <= qidx
        s1 = s_ref[slot, 0:nk, sl]
        if diag:
            s1 = jnp.where(causal, s1, -jnp.inf)
        m_old = m_ref[:, sl]
        m_new = jnp.maximum(m_old, jnp.max(s1, axis=0, keepdims=True) + off)
        m_ref[:, sl] = m_new
        s2 = s_ref[slot, 0:nk, sl] - (m_new - off)
        if diag:
            s2 = jnp.where(causal, s2, -jnp.inf)
        p = jnp.exp2(s2)
        alpha = jnp.exp2(m_old - m_new)
        p_ref[pslot, 0:nk, sl] = p.astype(BF16)
        if nk < t:
            p_ref[pslot, nk:t, sl] = jnp.zeros((t - nk, LANES), BF16)
        return alpha

    def run(js, diag_last, ahead):
        order = list(js) + list(ahead)
        pending = None
        for n, j in enumerate(js):
            if n + 2 < len(order):
                put_scores(order[n + 2], (n + 2) % 4)
            diag = diag_last and n == len(js) - 1
            off = (slope * LOG2E) * jnp.full((1, 1), (j - i) * t, jnp.int32).astype(F32)
            alpha = jnp.concatenate([strip(n % 4, n % 2, c, off, diag) for c in range(2 * ns)], axis=1)
            pv = _dot(vt_ref[j], p_ref[n % 2])
            if pending is not None:
                acc_ref[...] = pending[0] * acc_ref[...] + pending[1]
            pending = (alpha, pv)
        acc_ref[...] = pending[0] * acc_ref[...] + pending[1]

    ng = i // unroll

    def body(g, carry):
        base = g * unroll
        run([base + u for u in range(unroll)], False, [base + unroll, jnp.minimum(base + unroll + 1, i)])
        return carry

    put_scores(0, 0)
    put_scores(jnp.minimum(1, i), 1)
    lax.fori_loop(0, ng, body, 0)
    for rem in range(unroll):

        @pl.when(i - ng * unroll == rem)
        def _(rem=rem):
            run([ng * unroll + u for u in range(rem)] + [i], True, [])

    lam = _lambda_full(lam_ref)
    o = acc_ref[0:hw, :] / acc_ref[hw:hw + 1, :]
    o = o[:, 0:t] - lam * o[:, t:2 * t]
    var = jnp.mean(o * o, axis=0, keepdims=True)
    o = o * lax.rsqrt(var + EPS) * subt_ref[:, 0:1] * (1.0 - LAMBDA_INIT)
    for c in range(ns):
        o_ref[c * LANES:(c + 1) * LANES, :] = o[:, c * LANES:(c + 1) * LANES].T.astype(o_ref.dtype)


def attn_prompt(qkv, nseq, seq, t, lam4, subln):
    nq = seq // t
    hw = 2 * ATT_HEAD_DIM
    qkv3 = qkv.reshape(nseq, seq, qkv.shape[1])
    slopes = jnp.asarray(np.tile(_alibi_slopes()[:, None], (1, LANES)))
    subt = jnp.tile(subln.reshape(hw, 1), (1, LANES))
    return pl.pallas_call(
        functools.partial(_attn_prompt_kernel, t=t, unroll=4),
        grid=(nseq, ATT_HEADS, nq),
        in_specs=[pl.BlockSpec((None, t, hw), lambda b, h, i: (b, i, h)),
                  pl.BlockSpec((None, seq, hw), lambda b, h, i: (b, 0, ATT_HEADS + h)),
                  pl.BlockSpec((None, seq, hw), lambda b, h, i: (b, 0, 2 * ATT_HEADS + h)),
                  _resident(slopes.shape), _resident(lam4.shape), _resident(subt.shape)],
        out_specs=pl.BlockSpec((None, t, hw), lambda b, h, i: (b, i, h)),
        out_shape=jax.ShapeDtypeStruct((nseq, seq, ATT_HEADS * ATT_V_DIM), BF16),
        scratch_shapes=[pltpu.VMEM((nq, hw + BF16_ROWS, t), BF16), pltpu.VMEM((t, 2 * t), F32),
                        pltpu.VMEM((4, t, 2 * t), F32), pltpu.VMEM((2, t, 2 * t), BF16),
                        pltpu.VMEM((1, 2 * t), F32), pltpu.VMEM((hw + BF16_ROWS, 2 * t), F32)],
        compiler_params=_cparams(("parallel", "parallel", "arbitrary")),
        name="attn_prompt",
    )(qkv3, qkv3, qkv3, slopes, lam4, subt).reshape(nseq * seq, ATT_HEADS * ATT_V_DIM)


def _attn_sample_kernel(pt_ref, *refs, npp, past):
    k_refs = refs[:npp]
    v_refs = refs[npp:2 * npp]
    (q_ref, knew_ref, vnew_ref, slope_ref, lam_ref, sub_ref, o_ref, b0_ref, m_ref, l_ref, acc_ref) = refs[2 * npp:]
    j = pl.program_id(1)
    nj = pl.num_programs(1)
    nr = q_ref.shape[0]
    ncol = PAGE_SIZE * ATT_HEADS
    rowi = lax.broadcasted_iota(jnp.int32, (nr, 1), 0)
    slope = slope_ref[:, 0:1] * LOG2E
    tq = (rowi % TOK0).astype(F32)

    @pl.when(j == 0)
    def _():
        m_ref[...] = jnp.full_like(m_ref, -jnp.inf)
        l_ref[...] = jnp.zeros_like(l_ref)
        acc_ref[...] = jnp.zeros_like(acc_ref)
        col = lax.broadcasted_iota(jnp.int32, (1, ncol), 1)
        same_head = (col % ATT_HEADS) == (rowi // GROUP)
        b0_ref[...] = jnp.where(same_head, slope * ((col // ATT_HEADS).astype(F32) - tq), -jnp.inf)

    q = q_ref[...]
    ss = [_dot_nt(q, k_refs[r][...].astype(BF16)) for r in range(npp)]
    stats = []
    for s in ss:
        s = s + b0_ref[...]
        mr = jnp.max(s, axis=-1, keepdims=True)
        p = jnp.exp2(s - mr)
        stats.append((mr, jnp.sum(p, axis=-1, keepdims=True), p.astype(BF16)))
    parts = []
    for r, (mr, lr, p) in enumerate(stats):
        pv = _dot(p, v_refs[r][...].astype(BF16))
        page0 = jnp.full((1, 1), (j * npp + r) * PAGE_SIZE, jnp.int32).astype(F32)
        parts.append((mr + slope * (page0 - past), lr, pv))
    m_old = m_ref[...]
    m_new = m_old
    for mt, _, _ in parts:
        m_new = jnp.maximum(m_new, mt)
    alpha = jnp.exp2(m_old - m_new)
    l_acc = alpha * l_ref[...]
    acc = alpha * acc_ref[...]
    for mt, lr, pv in parts:
        wgt = jnp.exp2(mt - m_new)
        l_acc = l_acc + wgt * lr
        acc = acc + wgt * pv
    m_ref[...] = m_new
    l_ref[...] = l_acc
    acc_ref[...] = acc

    @pl.when(j == nj - 1)
    def _():
        nn = knew_ref.shape[0]
        zpad = jnp.zeros((LANES - nn, LANES), F32)
        kb = jnp.concatenate([knew_ref[...], zpad], axis=0).astype(BF16)
        vb = jnp.concatenate([vnew_ref[...], zpad], axis=0).astype(BF16)
        col = lax.broadcasted_iota(jnp.int32, (1, LANES), 1)
        grow = col // ATT_HEADS
        tk = (grow - TOK0).astype(F32)
        ok = jnp.logical_and(jnp.logical_and(grow >= TOK0, grow < GROUP), tk <= tq)
        ok = jnp.logical_and(ok, (col % ATT_HEADS) == (rowi // GROUP))
        s = jnp.where(ok, _dot_nt(q, kb) - slope * (tq - tk), -jnp.inf)
        m_o = m_ref[...]
        m_n = jnp.maximum(m_o, jnp.max(s, axis=-1, keepdims=True))
        p = jnp.exp2(s - m_n)
        al = jnp.exp2(m_o - m_n)
        l_fin = al * l_ref[...] + jnp.sum(p, axis=-1, keepdims=True)
        acc_fin = (al * acc_ref[...] + _dot(p.astype(BF16), vb)) / l_fin

        lam = _lambda_full(lam_ref)
        rows8 = lax.broadcasted_iota(jnp.int32, (GROUP, 1), 0)
        for h in range(ATT_HEADS):
            blk = acc_fin[h * GROUP:(h + 1) * GROUP, :]
            o = blk - lam * pltpu.roll(blk, TOK0, axis=0)
            o = _rms(o, sub_ref[...]) * (1.0 - LAMBDA_INIT)
            o_ref[:, h * ATT_V_DIM:(h + 1) * ATT_V_DIM] = jnp.where(rows8 >= TOK0, o, 0.0).astype(o_ref.dtype)


def attn_sample(page_table, cache_k, cache_v, qh, k_new, v_new, lam4, subln, npp):
    nb, n_pages = page_table.shape
    n_pool = cache_k.shape[0]
    dk = ATT_HEADS * ATT_V_DIM
    ncol = PAGE_SIZE * ATT_HEADS
    ck = cache_k.reshape(n_pool, ncol, 2 * ATT_HEAD_DIM)
    cv = cache_v.reshape(n_pool, ncol, ATT_V_DIM)
    nr = qh.shape[1]
    nn = GROUP * ATT_HEADS
    slopes = jnp.asarray(np.tile(np.repeat(_alibi_slopes(), GROUP)[:, None], (1, LANES)))

    def page_spec(r):
        return pl.BlockSpec((None, ncol, LANES), lambda b, j, pt: (pt[b, j * npp + r], 0, 0))

    in_specs = [page_spec(r) for r in range(npp)] + [page_spec(r) for r in range(npp)]
    in_specs += [pl.BlockSpec((None, nr, LANES), lambda b, j, pt: (b, 0, 0)),
                 pl.BlockSpec((nn, LANES), lambda b, j, pt: (b, 0)),
                 pl.BlockSpec((nn, LANES), lambda b, j, pt: (b, 0)),
                 pl.BlockSpec(slopes.shape, lambda b, j, pt: (0, 0)),
                 pl.BlockSpec(lam4.shape, lambda b, j, pt: (0, 0)),
                 pl.BlockSpec(subln.shape, lambda b, j, pt: (0, 0))]
    grid_spec = pltpu.PrefetchScalarGridSpec(
        num_scalar_prefetch=1,
        grid=(nb, n_pages // npp),
        in_specs=in_specs,
        out_specs=pl.BlockSpec((GROUP, dk), lambda b, j, pt: (b, 0)),
        scratch_shapes=[pltpu.VMEM((nr, ncol), F32), pltpu.VMEM((nr, 1), F32), pltpu.VMEM((nr, 1), F32),
                        pltpu.VMEM((nr, LANES), F32)],
    )
    return pl.pallas_call(
        functools.partial(_attn_sample_kernel, npp=npp, past=float(n_pages * PAGE_SIZE)),
        grid_spec=grid_spec,
        out_shape=jax.ShapeDtypeStruct((nb * GROUP, dk), BF16),
        compiler_params=_cparams(("parallel", "arbitrary")),
        name="attn_sample",
    )(page_table, *([ck] * npp), *([cv] * npp), qh, k_new.reshape(nb * nn, LANES), v_new.reshape(nb * nn, LANES),
      slopes, lam4, subln)


def _block_diag4(w):
    nb = w.shape[0] // 4
    eye = jnp.eye(4, dtype=w.dtype)
    w4 = w.reshape(nb, 4, RNN_BW, RNN_BW)
    return jnp.einsum('cajk,ab->cajbk', w4, eye).reshape(nb, 4 * RNN_BW, 4 * RNN_BW).astype(BF16)


def _prep_weights(norm_mix, norm_ffn, norm_final, w_in0, conv_rnn_w, conv_rnn_b, rg_w_a, rg_b_a, rg_w_x, rg_b_x,
                  rg_lambda, conv_ssm_w, conv_ssm_b, dt_bias, a_log, d_skip, ssm_norm, w_out0, w_in1, lambda_q1,
                  lambda_k1, lambda_q2, lambda_k2, subln, w_out1, w_up, ffn_conv_w, ffn_conv_b, w_down):
    row = lambda v: v.reshape(1, -1)
    pad_l = lambda v: jnp.pad(v.reshape(1, -1), ((0, 0), (0, LANES - v.shape[-1])))
    return dict(
        norm_mix=[row(norm_mix[l]) for l in range(2)],
        norm_ffn=[row(norm_ffn[l]) for l in range(2)],
        norm_final=row(norm_final),
        w_in0=jnp.pad(w_in0, ((0, 0), (0, D_IN0_PAD - D_IN0))).astype(BF16),
        conv_rnn_w=conv_rnn_w, conv_rnn_b=row(conv_rnn_b),
        wa=_block_diag4(rg_w_a), ba=row(rg_b_a), wx=_block_diag4(rg_w_x), bx=row(rg_b_x), lam=row(rg_lambda),
        conv_ssm_w=conv_ssm_w, conv_ssm_b=row(conv_ssm_b),
        dt_bias=pad_l(dt_bias), a_log=pad_l(a_log),
        d_skip=row(jnp.repeat(d_skip, SSM_HEAD_DIM)), ssm_norm=row(ssm_norm),
        w_out0a=w_out0[:D_RNN].astype(BF16), w_out0b=w_out0[D_RNN:].astype(BF16),
        w_in1=w_in1.astype(BF16),
        lam4=jnp.stack([lambda_q1, lambda_k1, lambda_q2, lambda_k2]), subln=row(subln),
        w_out1=w_out1.astype(BF16),
        w_up=[w_up[l].astype(BF16) for l in range(2)],
        ffn_conv_w=[ffn_conv_w[l] for l in range(2)], ffn_conv_b=[row(ffn_conv_b[l]) for l in range(2)],
        w_down=[w_down[l].astype(BF16) for l in range(2)],
    )


def _state_t(s):
    b = s.shape[0]
    return jnp.transpose(s, (0, 3, 1, 2)).reshape(b, SSM_STATE, D_SSM)


def _state_untranspose(st):
    b = st.shape[0]
    return jnp.transpose(st.reshape(b, SSM_STATE, SSM_HEADS, SSM_HEAD_DIM), (0, 2, 3, 1))


def _prompt_trunk(x_prompt, w):
    bsz, seq, d = x_prompt.shape
    m = bsz * seq
    x = x_prompt.reshape(m, d)
    tm = min(512, seq)
    tr = min(256, seq)

    p0 = proj0(x, w['norm_mix'][0], w['w_in0'], tm)
    rnn_out, h_tail = rglru(p0, bsz, seq, tr, w['conv_rnn_w'], w['conv_rnn_b'], w['wa'], w['ba'], w['wx'],
                            w['bx'], w['lam'])
    lc = SSD_CHUNK
    nch = seq // lc
    hb = lc // HALO
    xbc_cb = (2 * D_RNN + D_SSM) // SSM_CONV_DIM
    halo_spec = pl.BlockSpec(
        (HALO, SSM_CONV_DIM), lambda b, i: (jnp.maximum((b * nch + i) * hb - 1, 0), xbc_cb))
    y_ssm, st = ssd(p0, xbc_cb, p0, halo_spec, p0, 2 * D_RNN // D_SSM, p0, (D_IN0_PAD - LANES) // LANES,
                    bsz, nch, lc, lc, w['conv_ssm_w'], w['conv_ssm_b'], w['dt_bias'], w['a_log'],
                    w['d_skip'], w['ssm_norm'])
    x = out_proj([rnn_out, y_ssm], [w['w_out0a'], w['w_out0b']], x, tm)
    x, u0 = ffn(x, bsz, seq, tm, w['norm_ffn'][0], w['w_up'][0], w['ffn_conv_w'][0], w['ffn_conv_b'][0],
                w['w_down'][0], w['norm_final'], False)

    k, v, qkv = proj1(x, w['norm_mix'][1], w['w_in1'], tm)
    o = attn_prompt(qkv, bsz, seq, min(256, seq), w['lam4'], w['subln'])
    x = out_proj([o], [w['w_out1']], x, tm)
    y, u1 = ffn(x, bsz, seq, tm, w['norm_ffn'][1], w['w_up'][1], w['ffn_conv_w'][1], w['ffn_conv_b'][1],
                w['w_down'][1], w['norm_final'], True)

    p3 = p0.reshape(bsz, seq, D_IN0_PAD)
    rnn_conv = p3[:, seq - (CONV_W - 1):, 0:D_RNN]
    ssm_conv = p3[:, seq - (CONV_W - 1):, 2 * D_RNN + D_SSM:2 * D_RNN + D_SSM + SSM_CONV_DIM]
    ffn_conv = jnp.stack([u0[:, SUBLANES - (FFN_CONV_W - 1):], u1[:, SUBLANES - (FFN_CONV_W - 1):]])
    return (y.reshape(bsz, seq, d), rnn_conv, h_tail[:, SUBLANES - 1], ssm_conv, _state_untranspose(st),
            k.reshape(bsz, seq, ATT_HEADS, 2 * ATT_HEAD_DIM), v.reshape(bsz, seq, ATT_HEADS, ATT_V_DIM), ffn_conv)


def _to_groups(x_tok, hist=None):
    b, t, c = x_tok.shape
    if hist is None:
        lead = jnp.zeros((b, TOK0, c), x_tok.dtype)
    else:
        k = hist.shape[1]
        lead = jnp.concatenate([jnp.zeros((b, TOK0 - k, c), x_tok.dtype), hist], axis=1)
    return jnp.concatenate([lead, x_tok], axis=1).reshape(b * GROUP, c)


def _sample_trunk(x_sample, state_rnn_conv, state_rnn_h, state_ssm_conv, state_ssm, cache_k, cache_v,
                  state_ffn_conv, page_table, w):
    nb, t, d = x_sample.shape
    assert t == GROUP - TOK0
    m = nb * GROUP
    x = _to_groups(x_sample)

    p0 = proj0(x, w['norm_mix'][0], w['w_in0'], m)
    p0g = p0.reshape(nb, GROUP, D_IN0_PAD)
    c_xbc = 2 * D_RNN + D_SSM
    xr = _to_groups(p0g[:, TOK0:, 0:D_RNN], state_rnn_conv)
    p0r = jnp.concatenate([xr, p0[:, D_RNN:2 * D_RNN]], axis=1)
    hinj = _to_groups(jnp.zeros((nb, t, D_RNN), F32), state_rnn_h[:, None, :])
    rnn_out, h_all = rglru(p0r, 1, m, m, w['conv_rnn_w'], w['conv_rnn_b'], w['wa'], w['ba'], w['wx'], w['bx'],
                           w['lam'], hinj=hinj)
    lc = SSD_CHUNK
    padc = lambda a: jnp.pad(a, ((0, 0), (0, lc - t), (0, 0))).reshape(nb * lc, a.shape[-1])
    xbc_p = padc(p0g[:, TOK0:, c_xbc:c_xbc + SSM_CONV_DIM])
    z_p = padc(p0g[:, TOK0:, 2 * D_RNN:2 * D_RNN + D_SSM])
    dt_p = padc(p0g[:, TOK0:, D_IN0_PAD - LANES:])
    halo = jnp.concatenate([jnp.zeros((nb, HALO - (CONV_W - 1), SSM_CONV_DIM), F32), state_ssm_conv], axis=1)
    halo_spec = pl.BlockSpec((None, HALO, SSM_CONV_DIM), lambda b, i: (b, 0, 0))
    y_p, st = ssd(xbc_p, 0, halo, halo_spec, z_p, 0, dt_p, 0, nb, 1, lc, t, w['conv_ssm_w'], w['conv_ssm_b'],
                  w['dt_bias'], w['a_log'], w['d_skip'], w['ssm_norm'], s0=_state_t(state_ssm))
    y_ssm = _to_groups(y_p.reshape(nb, lc, D_SSM)[:, 0:t])
    x = out_proj([rnn_out, y_ssm], [w['w_out0a'], w['w_out0b']], x, m)
    uinj0 = _to_groups(jnp.zeros((nb, t, 2 * D_FF), F32), state_ffn_conv[0])
    x, u0 = ffn(x, 1, m, m, w['norm_ffn'][0], w['w_up'][0], w['ffn_conv_w'][0], w['ffn_conv_b'][0],
                w['w_down'][0], w['norm_final'], False, uinj=uinj0)

    k, v, qkv = proj1(x, w['norm_mix'][1], w['w_in1'], m)
    qg = qkv[:, 0:D_MODEL].reshape(nb, GROUP, ATT_HEADS, 2, ATT_HEAD_DIM)[:, TOK0:]
    flip = jnp.array([[0, 1], [1, 0]], dtype=BF16)
    qh = jnp.einsum('bqhmd,nm->bhnqmd', qg, flip).reshape(nb, ATT_HEADS * GROUP, 2 * ATT_HEAD_DIM)
    o = attn_sample(page_table, cache_k, cache_v, qh, k, v, w['lam4'], w['subln'], npp=8)
    x = out_proj([o], [w['w_out1']], x, m)
    uinj1 = _to_groups(jnp.zeros((nb, t, 2 * D_FF), F32), state_ffn_conv[1])
    y, u1 = ffn(x, 1, m, m, w['norm_ffn'][1], w['w_up'][1], w['ffn_conv_w'][1], w['ffn_conv_b'][1],
                w['w_down'][1], w['norm_final'], True, uinj=uinj1)

    tok = lambda a: a.reshape(nb, GROUP, a.shape[-1])[:, TOK0:]
    rnn_conv = p0g[:, GROUP - (CONV_W - 1):, 0:D_RNN]
    ssm_conv = p0g[:, GROUP - (CONV_W - 1):, c_xbc:c_xbc + SSM_CONV_DIM]
    ffn_conv = jnp.stack([u0.reshape(nb, GROUP, -1)[:, GROUP - (FFN_CONV_W - 1):],
                          u1.reshape(nb, GROUP, -1)[:, GROUP - (FFN_CONV_W - 1):]])
    return (tok(y), rnn_conv, h_all.reshape(nb, GROUP, D_RNN)[:, GROUP - 1], ssm_conv, _state_untranspose(st),
            tok(k).reshape(nb, t, ATT_HEADS, 2 * ATT_HEAD_DIM), tok(v).reshape(nb, t, ATT_HEADS, ATT_V_DIM),
            ffn_conv)


def kernel(x_prompt, x_sample, state_rnn_conv, state_rnn_h, state_ssm_conv, state_ssm, cache_k, cache_v, state_ffn_conv, page_table, norm_mix, norm_ffn, norm_final, w_in0, conv_rnn_w, conv_rnn_b, rg_w_a, rg_b_a, rg_w_x, rg_b_x, rg_lambda, conv_ssm_w, conv_ssm_b, dt_bias, a_log, d_skip, ssm_norm, w_out0, w_in1, lambda_q1, lambda_k1, lambda_q2, lambda_k2, subln, w_out1, w_up, ffn_conv_w, ffn_conv_b, w_down):
    w = _prep_weights(norm_mix, norm_ffn, norm_final, w_in0, conv_rnn_w, conv_rnn_b, rg_w_a, rg_b_a, rg_w_x,
                      rg_b_x, rg_lambda, conv_ssm_w, conv_ssm_b, dt_bias, a_log, d_skip, ssm_norm, w_out0, w_in1,
                      lambda_q1, lambda_k1, lambda_q2, lambda_k2, subln, w_out1, w_up, ffn_conv_w, ffn_conv_b,
                      w_down)
    p = _prompt_trunk(x_prompt, w)
    s = _sample_trunk(x_sample, state_rnn_conv, state_rnn_h, state_ssm_conv, state_ssm, cache_k, cache_v,
                      state_ffn_conv, page_table, w)
    return (p[0], s[0]) + p[1:] + s[1:]
```

```python
import functools
import math

import jax
import jax.numpy as jnp
import numpy as np
from jax import lax
from jax.experimental import pallas as pl
from jax.experimental.pallas import tpu as pltpu

F32 = jnp.float32
BF16 = jnp.bfloat16

D_MODEL = 1024
D_RNN = 1024
RNN_BW = 64
RG_C = 8.0
CONV_W = 4
SSM_HEADS = 16
SSM_HEAD_DIM = 64
D_SSM = SSM_HEADS * SSM_HEAD_DIM
SSM_GROUPS = 2
SSM_STATE = 128
SSM_CONV_DIM = D_SSM + 2 * SSM_GROUPS * SSM_STATE
SSD_CHUNK = 128
D_IN0 = 2 * D_RNN + D_SSM + SSM_CONV_DIM + SSM_HEADS
D_IN0_PAD = 2 * D_RNN + D_SSM + SSM_CONV_DIM + 128
ATT_HEADS = 8
ATT_HEAD_DIM = 64
ATT_V_DIM = 128
LAMBDA_INIT = 0.8 - 0.6 * math.exp(-0.3 * 1)
D_FF = 3 * D_MODEL
FFN_CONV_W = 3
EPS = 1e-6
LOG2E = math.log2(math.e)
PAGE_SIZE = 128

SUBLANES = 8
BF16_ROWS = 16
LANES = 128
HALO = SUBLANES
GROUP = 8
TOK0 = 4
VMEM_LIMIT = 56 * 1024 * 1024


def _cparams(sem):
    return pltpu.CompilerParams(dimension_semantics=sem, vmem_limit_bytes=VMEM_LIMIT)


def _resident(shape):
    nd = len(shape)
    return pl.BlockSpec(shape, lambda *_: (0,) * nd, pipeline_mode=pl.Buffered(1))


def _rms(x, g):
    var = jnp.mean(x * x, axis=-1, keepdims=True)
    return x * lax.rsqrt(var + EPS) * g


def _softplus(x):
    return jnp.maximum(x, 0.0) + jnp.log1p(jnp.exp(-jnp.abs(x)))


def _dot(a, b):
    return jnp.dot(a, b, preferred_element_type=F32)


def _dot_nt(a, b):
    return lax.dot_general(a, b, (((1,), (1,)), ((), ())), preferred_element_type=F32)


def _proj0_kernel(x_ref, g_ref, w_ref, o_ref, *, tn):
    xn = _rms(x_ref[...], g_ref[...]).astype(BF16)
    n = w_ref.shape[1]
    for c in range(0, n, tn):
        w = min(tn, n - c)
        o_ref[:, c:c + w] = _dot(xn, w_ref[:, c:c + w])


def proj0(x, g, w, tm):
    m, d = x.shape
    n = w.shape[1]
    return pl.pallas_call(
        functools.partial(_proj0_kernel, tn=512),
        grid=(m // tm,),
        in_specs=[pl.BlockSpec((tm, d), lambda i: (i, 0)), _resident((1, d)), _resident((d, n))],
        out_specs=pl.BlockSpec((tm, n), lambda i: (i, 0)),
        out_shape=jax.ShapeDtypeStruct((m, n), F32),
        compiler_params=_cparams(("parallel",)),
        name="proj0",
    )(x, g, w)


def _proj1_kernel(x_ref, g_ref, w_ref, k_ref, v_ref, qkv_ref, *, tn):
    xn = _rms(x_ref[...], g_ref[...]).astype(BF16)
    d = k_ref.shape[1]
    for c in range(0, 3 * d, tn):
        r = _dot(xn, w_ref[:, c:c + tn])
        if c < d:
            qkv_ref[:, c:c + tn] = (r * (ATT_HEAD_DIM ** -0.5 * LOG2E)).astype(BF16)
        else:
            qkv_ref[:, c:c + tn] = r.astype(BF16)
            if c < 2 * d:
                k_ref[:, c - d:c - d + tn] = r
            else:
                v_ref[:, c - 2 * d:c - 2 * d + tn] = r


def proj1(x, g, w, tm):
    m, d = x.shape
    n = w.shape[1]
    return pl.pallas_call(
        functools.partial(_proj1_kernel, tn=512),
        grid=(m // tm,),
        in_specs=[pl.BlockSpec((tm, d), lambda i: (i, 0)), _resident((1, d)), _resident((d, n))],
        out_specs=[pl.BlockSpec((tm, d), lambda i: (i, 0)), pl.BlockSpec((tm, d), lambda i: (i, 0)),
                   pl.BlockSpec((tm, n), lambda i: (i, 0))],
        out_shape=[jax.ShapeDtypeStruct((m, d), F32), jax.ShapeDtypeStruct((m, d), F32),
                   jax.ShapeDtypeStruct((m, n), BF16)],
        compiler_params=_cparams(("parallel",)),
        name="proj1",
    )(x, g, w)


def _out_proj_kernel(*refs, n_in, tn):
    a_refs = refs[:n_in]
    w_refs = refs[n_in:2 * n_in]
    res_ref = refs[2 * n_in]
    o_ref = refs[2 * n_in + 1]
    n = o_ref.shape[1]
    for c in range(0, n, tn):
        acc = res_ref[:, c:c + tn]
        for a_ref, w_ref in zip(a_refs, w_refs):
            acc = acc + _dot(a_ref[...], w_ref[:, c:c + tn])
        o_ref[:, c:c + tn] = acc


def out_proj(a_list, w_list, res, tm):
    m, n = res.shape
    n_in = len(a_list)
    in_specs = [pl.BlockSpec((tm, a.shape[1]), lambda i: (i, 0)) for a in a_list]
    in_specs += [_resident(w.shape) for w in w_list]
    in_specs += [pl.BlockSpec((tm, n), lambda i: (i, 0))]
    return pl.pallas_call(
        functools.partial(_out_proj_kernel, n_in=n_in, tn=512),
        grid=(m // tm,),
        in_specs=in_specs,
        out_specs=pl.BlockSpec((tm, n), lambda i: (i, 0)),
        out_shape=jax.ShapeDtypeStruct((m, n), F32),
        compiler_params=_cparams(("parallel",)),
        name="out_proj",
    )(*a_list, *w_list, res)


def _conv_from_ext(ext_ref, w_ref, b_ref, width, tm, c0, c1):
    y = b_ref[:, c0:c1]
    for j in range(width):
        off = HALO - (width - 1) + j
        y = y + ext_ref[off:off + tm, c0:c1] * w_ref[j:j + 1, c0:c1]
    return y


def _scan_rows(a, b, c):
    tm = a.shape[0]
    rowm = lax.broadcasted_iota(jnp.int32, a.shape, 0) % SUBLANES
    s = 1
    while s < SUBLANES:
        keep = rowm >= s
        a_s = jnp.where(keep, pltpu.roll(a, s, axis=0), 1.0)
        b_s = jnp.where(keep, pltpu.roll(b, s, axis=0), 0.0)
        b = a * b_s + b
        a = a * a_s
        s *= 2
    hs = []
    for g in range(tm // SUBLANES):
        hg = a[g * SUBLANES:(g + 1) * SUBLANES] * c + b[g * SUBLANES:(g + 1) * SUBLANES]
        c = hg[SUBLANES - 1:SUBLANES, :]
        hs.append(hg)
    return jnp.concatenate(hs, axis=0)


def _rglru_kernel(*refs, tm, sample):
    if sample:
        (x_ref, y_ref, cw_ref, cb_ref, wa_ref, ba_ref, wx_ref, bx_ref, lam_ref, hinj_ref,
         out_ref, h_out_ref, ext_ref, carry_ref) = refs
    else:
        (x_ref, y_ref, halo_ref, cw_ref, cb_ref, wa_ref, ba_ref, wx_ref, bx_ref, lam_ref,
         out_ref, h_out_ref, ext_ref, carry_ref) = refs
    i = pl.program_id(1)
    d = x_ref.shape[1]

    @pl.when(i == 0)
    def _():
        carry_ref[...] = jnp.zeros_like(carry_ref)

    if sample:
        ext_ref[0:HALO, :] = jnp.zeros((HALO, d), F32)
    else:
        ext_ref[0:HALO, :] = jnp.where(i == 0, 0.0, halo_ref[...])
    ext_ref[HALO:HALO + tm, :] = x_ref[...]

    row = lax.broadcasted_iota(jnp.int32, (tm, 1), 0)
    tail = h_out_ref.shape[0]
    sp = _softplus(-lam_ref[...])
    cbw = 4 * RNN_BW
    for c in range(d // cbw):
        c0, c1 = c * cbw, (c + 1) * cbw
        xr = _conv_from_ext(ext_ref, cw_ref, cb_ref, CONV_W, tm, c0, c1)
        xb = xr.astype(BF16)
        r = jax.nn.sigmoid(_dot(xb, wa_ref[c]) + ba_ref[:, c0:c1])
        ig = jax.nn.sigmoid(_dot(xb, wx_ref[c]) + bx_ref[:, c0:c1])
        log_a = -RG_C * r * sp[:, c0:c1]
        a = jnp.exp(log_a)
        mult = jnp.sqrt(jnp.maximum(1.0 - a * a, 0.0))
        if sample:
            inj = (row % GROUP) == (TOK0 - 1)
            a = jnp.where(inj, 0.0, a)
            bv = jnp.where(inj, hinj_ref[:, c0:c1], xr * ig * mult)
        else:
            first = jnp.logical_and(i == 0, row == 0)
            a = jnp.where(first, 0.0, a)
            mult = jnp.where(first, 1.0, mult)
            bv = xr * ig * mult
        for k in range(cbw // LANES):
            l0, l1 = c0 + k * LANES, c0 + (k + 1) * LANES
            h = _scan_rows(a[:, k * LANES:(k + 1) * LANES], bv[:, k * LANES:(k + 1) * LANES], carry_ref[:, l0:l1])
            carry_ref[:, l0:l1] = h[tm - 1:tm, :]
            out_ref[:, l0:l1] = (h * jax.nn.gelu(y_ref[:, l0:l1])).astype(out_ref.dtype)
            h_out_ref[:, l0:l1] = h[tm - tail:tm, :]


def rglru(proj, nseq, seq, tm, cw, cb, wa, ba, wx, bx, lam, hinj=None):
    sample = hinj is not None
    nt = seq // tm
    d = D_RNN
    xmap = lambda b, i: (b * nt + i, 0)
    in_specs = [pl.BlockSpec((tm, d), xmap), pl.BlockSpec((tm, d), lambda b, i: (b * nt + i, 1))]
    args = [proj, proj]
    if not sample:
        hb = tm // HALO
        in_specs.append(pl.BlockSpec((HALO, d), lambda b, i: (jnp.maximum((b * nt + i) * hb - 1, 0), 0)))
        args.append(proj)
    in_specs += [_resident(cw.shape), _resident(cb.shape), _resident(wa.shape), _resident(ba.shape),
                 _resident(wx.shape), _resident(bx.shape), _resident(lam.shape)]
    args += [cw, cb, wa, ba, wx, bx, lam]
    if sample:
        in_specs.append(pl.BlockSpec((tm, d), xmap))
        args.append(hinj)
        h_spec = pl.BlockSpec((tm, d), xmap)
        h_shape = jax.ShapeDtypeStruct((nseq * seq, d), F32)
    else:
        h_spec = pl.BlockSpec((None, SUBLANES, d), lambda b, i: (b, 0, 0))
        h_shape = jax.ShapeDtypeStruct((nseq, SUBLANES, d), F32)
    return pl.pallas_call(
        functools.partial(_rglru_kernel, tm=tm, sample=sample),
        grid=(nseq, nt),
        in_specs=in_specs,
        out_specs=[pl.BlockSpec((tm, d), xmap), h_spec],
        out_shape=[jax.ShapeDtypeStruct((nseq * seq, d), BF16), h_shape],
        scratch_shapes=[pltpu.VMEM((tm + HALO, d), F32), pltpu.VMEM((1, d), F32)],
        compiler_params=_cparams(("parallel", "arbitrary")),
        name="rglru",
    )(*args)


def _pair(v, hp, lane_lo):
    return jnp.where(lane_lo, v[:, 2 * hp:2 * hp + 1], v[:, 2 * hp + 1:2 * hp + 2])


def _ssd_kernel(*refs, lc, n_valid, has_init):
    if has_init:
        (xbc_ref, halo_ref, z_ref, dt_ref, cw_ref, cb_ref, dtb_ref, alog_ref, dsk_ref, nrm_ref, s0_ref,
         y_ref, sout_ref, ext_ref, st_ref, yacc_ref) = refs
    else:
        (xbc_ref, halo_ref, z_ref, dt_ref, cw_ref, cb_ref, dtb_ref, alog_ref, dsk_ref, nrm_ref,
         y_ref, sout_ref, ext_ref, st_ref, yacc_ref) = refs
    i = pl.program_id(1)
    n = SSM_STATE
    hd2 = 2 * SSM_HEAD_DIM
    gw = D_SSM // SSM_GROUPS

    @pl.when(i == 0)
    def _():
        if has_init:
            st_ref[...] = s0_ref[...]
        else:
            st_ref[...] = jnp.zeros_like(st_ref)

    if has_init:
        ext_ref[0:HALO, :] = halo_ref[...]
    else:
        ext_ref[0:HALO, :] = jnp.where(i == 0, 0.0, halo_ref[...])
    ext_ref[HALO:HALO + lc, :] = xbc_ref[...]
    xbc = _conv_from_ext(ext_ref, cw_ref, cb_ref, CONV_W, lc, 0, SSM_CONV_DIM)
    xbc = xbc * jax.nn.sigmoid(xbc)
    xs = xbc[:, 0:D_SSM]

    row = lax.broadcasted_iota(jnp.int32, (lc, 1), 0)
    dt = _softplus(dt_ref[...] + dtb_ref[...])
    if n_valid < lc:
        dt = jnp.where(row < n_valid, dt, 0.0)
    acs = dt * (-jnp.exp(alog_ref[...]))
    s = 1
    while s < lc:
        acs = acs + jnp.where(row >= s, pltpu.roll(acs, s, axis=0), 0.0)
        s *= 2
    if lc == LANES:
        acs_t = acs.T
    else:
        acs_t = jnp.concatenate([acs, jnp.zeros((LANES - lc, LANES), F32)], axis=0).T[:, 0:lc]
    a_last = acs[lc - 1:lc, :]
    ea = jnp.exp(acs)
    dsv = jnp.exp(a_last - acs)
    cdec = jnp.exp(a_last)

    lane_lo = lax.broadcasted_iota(jnp.int32, (1, hd2), 1) < SSM_HEAD_DIM
    causal = (lax.broadcasted_iota(jnp.int32, (lc, lc), 0) >= lax.broadcasted_iota(jnp.int32, (lc, lc), 1))
    for g in range(SSM_GROUPS):
        bm = xbc[:, D_SSM + g * n:D_SSM + (g + 1) * n]
        cm = xbc[:, D_SSM + SSM_GROUPS * n + g * n:D_SSM + SSM_GROUPS * n + (g + 1) * n]
        bm_b = bm.astype(BF16)
        cm_b = cm.astype(BF16)
        cbm = _dot_nt(cm_b, bm_b)
        st_old = st_ref[:, g * gw:(g + 1) * gw]
        yoff = _dot(cm_b, st_old.astype(BF16))
        if lc == LANES:
            bm_t = bm.T.astype(BF16)
        else:
            bm_t = jnp.concatenate([bm, jnp.zeros((LANES - lc, n), F32)], axis=0).T[:, 0:lc].astype(BF16)
        w_parts = []
        cd_parts = []
        for pp in range(gw // hd2):
            hp = g * (gw // hd2) + pp
            h0, h1 = 2 * hp, 2 * hp + 1
            c0 = hp * hd2
            xg = xs[:, c0:c0 + hd2] * _pair(dt, hp, lane_lo)
            xg_b = xg.astype(BF16)
            yd = []
            for h in (h0, h1):
                seg = acs[:, h:h + 1] - acs_t[h:h + 1, :]
                dec = jnp.exp(jnp.where(causal, seg, -jnp.inf))
                yd.append(_dot((cbm * dec).astype(BF16), xg_b))
            ydiag = jnp.where(lane_lo, yd[0], yd[1])
            yo = yoff[:, pp * hd2:(pp + 1) * hd2] * _pair(ea, hp, lane_lo)
            yacc_ref[:, c0:c0 + hd2] = ydiag + yo + dsk_ref[:, c0:c0 + hd2] * xs[:, c0:c0 + hd2]
            w_parts.append((xg * _pair(dsv, hp, lane_lo)).astype(BF16))
            cd_parts.append(_pair(cdec, hp, lane_lo))
        w_all = jnp.concatenate(w_parts, axis=1)
        cd_all = jnp.concatenate(cd_parts, axis=1)
        st_ref[:, g * gw:(g + 1) * gw] = cd_all * st_old + _dot(bm_t, w_all)

    z = z_ref[...]
    y = yacc_ref[...] * (z * jax.nn.sigmoid(z))
    y_ref[...] = _rms(y, nrm_ref[...]).astype(y_ref.dtype)
    sout_ref[...] = st_ref[...]


def ssd(xbc_arr, xbc_col, halo_arr, halo_map, z_arr, z_col, dt_arr, dt_col, nseq, nchunk, lc, n_valid,
        cw, cb, dtb, alog, dsk, nrm, s0=None):
    has_init = s0 is not None
    rmap = lambda col: (lambda b, i: (b * nchunk + i, col))
    in_specs = [pl.BlockSpec((lc, SSM_CONV_DIM), rmap(xbc_col)),
                halo_map,
                pl.BlockSpec((lc, D_SSM), rmap(z_col)),
                pl.BlockSpec((lc, LANES), rmap(dt_col)),
                _resident(cw.shape), _resident(cb.shape), _resident(dtb.shape), _resident(alog.shape),
                _resident(dsk.shape), _resident(nrm.shape)]
    args = [xbc_arr, halo_arr, z_arr, dt_arr, cw, cb, dtb, alog, dsk, nrm]
    if has_init:
        in_specs.append(pl.BlockSpec((None, SSM_STATE, D_SSM), lambda b, i: (b, 0, 0)))
        args.append(s0)
    return pl.pallas_call(
        functools.partial(_ssd_kernel, lc=lc, n_valid=n_valid, has_init=has_init),
        grid=(nseq, nchunk),
        in_specs=in_specs,
        out_specs=[pl.BlockSpec((lc, D_SSM), lambda b, i: (b * nchunk + i, 0)),
                   pl.BlockSpec((None, SSM_STATE, D_SSM), lambda b, i: (b, 0, 0))],
        out_shape=[jax.ShapeDtypeStruct((nseq * nchunk * lc, D_SSM), BF16),
                   jax.ShapeDtypeStruct((nseq, SSM_STATE, D_SSM), F32)],
        scratch_shapes=[pltpu.VMEM((lc + HALO, SSM_CONV_DIM), F32), pltpu.VMEM((SSM_STATE, D_SSM), F32),
                        pltpu.VMEM((lc, D_SSM), F32)],
        compiler_params=_cparams(("parallel", "arbitrary")),
        name="ssd",
    )(*args)


def _ffn_kernel(*refs, tm, fc, sample, final_norm):
    if sample:
        (x_ref, g_ref, wup_ref, cw_ref, cb_ref, wdn_ref, gf_ref, uinj_ref,
         o_ref, u_out_ref, ext_ref) = refs
    else:
        (x_ref, halo_ref, g_ref, wup_ref, cw_ref, cb_ref, wdn_ref, gf_ref,
         o_ref, u_out_ref, ext_ref) = refs
    i = pl.program_id(1)
    d_ff = wdn_ref.shape[0]
    x = x_ref[...]
    hn = _rms(x, g_ref[...]).astype(BF16)
    if sample:
        row = lax.broadcasted_iota(jnp.int32, (tm, 1), 0) % GROUP
        hist = jnp.logical_and(row >= TOK0 - (FFN_CONV_W - 1), row < TOK0)
    else:
        halo_n = _rms(halo_ref[...], g_ref[...]).astype(BF16)
    tail = u_out_ref.shape[0]

    def up(c):
        out = []
        for base in (c, d_ff + c):
            u = _dot(hn, wup_ref[:, base:base + fc])
            uh = None if sample else _dot(halo_n, wup_ref[:, base:base + fc])
            out.append((u, uh))
        return out

    acc = x
    ups = up(0)
    for c in range(0, d_ff, fc):
        ups_next = up(c + fc) if c + fc < d_ff else None
        conv = []
        for n, base in enumerate((c, d_ff + c)):
            u, uh = ups[n]
            e_ref = ext_ref.at[n]
            if sample:
                u = jnp.where(hist, uinj_ref[:, base:base + fc], u)
                e_ref[0:HALO, :] = jnp.zeros((HALO, fc), F32)
            else:
                e_ref[0:HALO, :] = jnp.where(i == 0, 0.0, uh)
            e_ref[HALO:HALO + tm, :] = u
            u_out_ref[:, base:base + fc] = e_ref[HALO + tm - tail:HALO + tm, :]
            y = cb_ref[:, base:base + fc]
            for j in range(FFN_CONV_W):
                off = HALO - (FFN_CONV_W - 1) + j
                y = y + e_ref[off:off + tm, :] * cw_ref[j:j + 1, base:base + fc]
            conv.append(y)
        act = (jax.nn.gelu(conv[0]) * conv[1]).astype(BF16)
        acc = acc + _dot(act, wdn_ref[c:c + fc, :])
        ups = ups_next
    if final_norm:
        acc = _rms(acc, gf_ref[...])
    o_ref[...] = acc


def ffn(x, nseq, seq, tm, g, wup, cw, cb, wdn, gf, final_norm, uinj=None):
    sample = uinj is not None
    nt = seq // tm
    d = x.shape[1]
    d_up = wup.shape[1]
    xmap = lambda b, i: (b * nt + i, 0)
    in_specs = [pl.BlockSpec((tm, d), xmap)]
    args = [x]
    if not sample:
        hb = tm // HALO
        in_specs.append(pl.BlockSpec((HALO, d), lambda b, i: (jnp.maximum((b * nt + i) * hb - 1, 0), 0)))
        args.append(x)
    in_specs += [_resident(g.shape), _resident(wup.shape), _resident(cw.shape), _resident(cb.shape),
                 _resident(wdn.shape), _resident(gf.shape)]
    args += [g, wup, cw, cb, wdn, gf]
    if sample:
        in_specs.append(pl.BlockSpec((tm, d_up), xmap))
        args.append(uinj)
        u_spec = pl.BlockSpec((tm, d_up), xmap)
        u_shape = jax.ShapeDtypeStruct((nseq * seq, d_up), F32)
    else:
        u_spec = pl.BlockSpec((None, SUBLANES, d_up), lambda b, i: (b, 0, 0))
        u_shape = jax.ShapeDtypeStruct((nseq, SUBLANES, d_up), F32)
    fc = 512
    return pl.pallas_call(
        functools.partial(_ffn_kernel, tm=tm, fc=fc, sample=sample, final_norm=final_norm),
        grid=(nseq, nt),
        in_specs=in_specs,
        out_specs=[pl.BlockSpec((tm, d), xmap), u_spec],
        out_shape=[jax.ShapeDtypeStruct((nseq * seq, d), F32), u_shape],
        scratch_shapes=[pltpu.VMEM((2, tm + HALO, fc), F32)],
        compiler_params=_cparams(("parallel", "arbitrary")),
        name="ffn",
    )(*args)


def _lambda_full(lam_ref):
    lq1 = lam_ref[0:1, :]
    lk1 = lam_ref[1:2, :]
    lq2 = lam_ref[2:3, :]
    lk2 = lam_ref[3:4, :]
    return (jnp.exp(jnp.sum(lq1 * lk1, axis=-1, keepdims=True))
            - jnp.exp(jnp.sum(lq2 * lk2, axis=-1, keepdims=True)) + LAMBDA_INIT)


def _alibi_slopes():
    return np.asarray(2.0 ** (-8.0 * np.arange(1, ATT_HEADS + 1) / ATT_HEADS), dtype=np.float32)


def _attn_prompt_kernel(q_ref, k_ref, v_ref, slope_ref, lam_ref, subt_ref, o_ref, vt_ref, bias_ref, s_ref, p_ref,
                        m_ref, acc_ref, *, t, unroll):
    h = pl.program_id(1)
    i = pl.program_id(2)
    dh = ATT_HEAD_DIM
    hw = 2 * dh
    nkb = vt_ref.shape[0]
    ns = t // LANES

    @pl.when(i == 0)
    def _():
        for jb in range(nkb):
            for c in range(ns):
                r0 = jb * t + c * LANES
                vt_ref[jb, 0:hw, c * LANES:(c + 1) * LANES] = v_ref[r0:r0 + LANES, :].astype(F32).T.astype(BF16)
            ones_row = lax.broadcasted_iota(jnp.int32, (BF16_ROWS, t), 0) == 0
            vt_ref[jb, hw:hw + BF16_ROWS, :] = jnp.where(ones_row, 1.0, 0.0).astype(BF16)

    slope = slope_ref[pl.ds(h, 1), :][:, 0:1]

    @pl.when(i == 0)
    def _():
        kk = lax.broadcasted_iota(jnp.int32, (t, t), 0)
        qi = lax.broadcasted_iota(jnp.int32, (t, t), 1)
        b = (kk - qi).astype(F32) * (slope * LOG2E)
        bias_ref[...] = jnp.concatenate([b, b], axis=1)

    q = q_ref[...]
    lane = lax.broadcasted_iota(jnp.int32, (1, 2 * dh), 1)
    zero = jnp.zeros_like(q)
    qq = jnp.concatenate([jnp.where(lane < dh, q, zero), jnp.where(lane >= dh, q, zero)], axis=0)

    m_ref[...] = jnp.full_like(m_ref, -jnp.inf)
    acc_ref[...] = jnp.zeros_like(acc_ref)

    def put_scores(j, slot):
        s_ref[slot] = _dot_nt(k_ref[pl.ds(pl.multiple_of(j * t, t), t), :], qq) + bias_ref[...]

    def strip(slot, pslot, c, off, diag):
        cb = c % ns
        nk = (cb + 1) * LANES if diag else t
        sl = slice(c * LANES, (c + 1) * LANES)
        if diag:
            kidx = lax.broadcasted_iota(jnp.int32, (nk, LANES), 0)
            qidx = lax.broadcasted_iota(jnp.int32, (nk, LANES), 1) + cb * LANES
            causal = kidx <= qidx
        s1 = s_ref[slot, 0:nk, sl]
        if diag:
            s1 = jnp.where(causal, s1, -jnp.inf)
        m_old = m_ref[:, sl]
        m_new = jnp.maximum(m_old, jnp.max(s1, axis=0, keepdims=True) + off)
        m_ref[:, sl] = m_new
        s2 = s_ref[slot, 0:nk, sl] - (m_new - off)
        if diag:
            s2 = jnp.where(causal, s2, -jnp.inf)
        p = jnp.exp2(s2)
        alpha = jnp.exp2(m_old - m_new)
        p_ref[pslot, 0:nk, sl] = p.astype(BF16)
        if nk < t:
            p_ref[pslot, nk:t, sl] = jnp.zeros((t - nk, LANES), BF16)
        return alpha

    def run(js, diag_last, ahead):
        order = list(js) + list(ahead)
        pending = None
        for n, j in enumerate(js):
            if n + 2 < len(order):
                put_scores(order[n + 2], (n + 2) % 4)
            diag = diag_last and n == len(js) - 1
            off = (slope * LOG2E) * jnp.full((1, 1), (j - i) * t, jnp.int32).astype(F32)
            alpha = jnp.concatenate([strip(n % 4, n % 4, c, off, diag) for c in range(2 * ns)], axis=1)
            pv = _dot(vt_ref[j], p_ref[n % 4])
            if pending is not None:
                acc_ref[...] = pending[0] * acc_ref[...] + pending[1]
            pending = (alpha, pv)
        acc_ref[...] = pending[0] * acc_ref[...] + pending[1]

    ng = i // unroll

    def body(g, carry):
        base = g * unroll
        run([base + u for u in range(unroll)], False, [base + unroll, jnp.minimum(base + unroll + 1, i)])
        return carry

    put_scores(0, 0)
    put_scores(jnp.minimum(1, i), 1)
    lax.fori_loop(0, ng, body, 0)
    for rem in range(unroll):

        @pl.when(i - ng * unroll == rem)
        def _(rem=rem):
            run([ng * unroll + u for u in range(rem)] + [i], True, [])

    lam = _lambda_full(lam_ref)
    o = acc_ref[0:hw, :] / acc_ref[hw:hw + 1, :]
    o = o[:, 0:t] - lam * o[:, t:2 * t]
    var = jnp.mean(o * o, axis=0, keepdims=True)
    o = o * lax.rsqrt(var + EPS) * subt_ref[:, 0:1] * (1.0 - LAMBDA_INIT)
    for c in range(ns):
        o_ref[c * LANES:(c + 1) * LANES, :] = o[:, c * LANES:(c + 1) * LANES].T.astype(o_ref.dtype)


def attn_prompt(qkv, nseq, seq, t, lam4, subln):
    nq = seq // t
    hw = 2 * ATT_HEAD_DIM
    qkv3 = qkv.reshape(nseq, seq, qkv.shape[1])
    slopes = jnp.asarray(np.tile(_alibi_slopes()[:, None], (1, LANES)))
    subt = jnp.tile(subln.reshape(hw, 1), (1, LANES))
    return pl.pallas_call(
        functools.partial(_attn_prompt_kernel, t=t, unroll=4),
        grid=(nseq, ATT_HEADS, nq),
        in_specs=[pl.BlockSpec((None, t, hw), lambda b, h, i: (b, i, h)),
                  pl.BlockSpec((None, seq, hw), lambda b, h, i: (b, 0, ATT_HEADS + h)),
                  pl.BlockSpec((None, seq, hw), lambda b, h, i: (b, 0, 2 * ATT_HEADS + h)),
                  _resident(slopes.shape), _resident(lam4.shape), _resident(subt.shape)],
        out_specs=pl.BlockSpec((None, t, hw), lambda b, h, i: (b, i, h)),
        out_shape=jax.ShapeDtypeStruct((nseq, seq, ATT_HEADS * ATT_V_DIM), BF16),
        scratch_shapes=[pltpu.VMEM((nq, hw + BF16_ROWS, t), BF16), pltpu.VMEM((t, 2 * t), F32),
                        pltpu.VMEM((4, t, 2 * t), F32), pltpu.VMEM((4, t, 2 * t), BF16),
                        pltpu.VMEM((1, 2 * t), F32), pltpu.VMEM((hw + BF16_ROWS, 2 * t), F32)],
        compiler_params=_cparams(("parallel", "parallel", "arbitrary")),
        name="attn_prompt",
    )(qkv3, qkv3, qkv3, slopes, lam4, subt).reshape(nseq * seq, ATT_HEADS * ATT_V_DIM)


def _attn_sample_kernel(pt_ref, *refs, npp, past):
    k_refs = refs[:npp]
    v_refs = refs[npp:2 * npp]
    (q_ref, knew_ref, vnew_ref, slope_ref, lam_ref, sub_ref, o_ref, b0_ref, m_ref, l_ref, acc_ref) = refs[2 * npp:]
    j = pl.program_id(1)
    nj = pl.num_programs(1)
    nr = q_ref.shape[0]
    ncol = PAGE_SIZE * ATT_HEADS
    rowi = lax.broadcasted_iota(jnp.int32, (nr, 1), 0)
    slope = slope_ref[:, 0:1] * LOG2E
    tq = (rowi % TOK0).astype(F32)

    @pl.when(j == 0)
    def _():
        m_ref[...] = jnp.full_like(m_ref, -jnp.inf)
        l_ref[...] = jnp.zeros_like(l_ref)
        acc_ref[...] = jnp.zeros_like(acc_ref)
        col = lax.broadcasted_iota(jnp.int32, (1, ncol), 1)
        same_head = (col % ATT_HEADS) == (rowi // GROUP)
        b0_ref[...] = jnp.where(same_head, slope * ((col // ATT_HEADS).astype(F32) - tq), -jnp.inf)

    q = q_ref[...]
    ss = [_dot_nt(q, k_refs[r][...].astype(BF16)) for r in range(npp)]
    stats = []
    for s in ss:
        s = s + b0_ref[...]
        mr = jnp.max(s, axis=-1, keepdims=True)
        p = jnp.exp2(s - mr)
        stats.append((mr, jnp.sum(p, axis=-1, keepdims=True), p.astype(BF16)))
    parts = []
    for r, (mr, lr, p) in enumerate(stats):
        pv = _dot(p, v_refs[r][...].astype(BF16))
        page0 = jnp.full((1, 1), (j * npp + r) * PAGE_SIZE, jnp.int32).astype(F32)
        parts.append((mr + slope * (page0 - past), lr, pv))
    m_old = m_ref[...]
    m_new = m_old
    for mt, _, _ in parts:
        m_new = jnp.maximum(m_new, mt)
    alpha = jnp.exp2(m_old - m_new)
    l_acc = alpha * l_ref[...]
    acc = alpha * acc_ref[...]
    for mt, lr, pv in parts:
        wgt = jnp.exp2(mt - m_new)
        l_acc = l_acc + wgt * lr
        acc = acc + wgt * pv
    m_ref[...] = m_new
    l_ref[...] = l_acc
    acc_ref[...] = acc

    @pl.when(j == nj - 1)
    def _():
        nn = knew_ref.shape[0]
        zpad = jnp.zeros((LANES - nn, LANES), F32)
        kb = jnp.concatenate([knew_ref[...], zpad], axis=0).astype(BF16)
        vb = jnp.concatenate([vnew_ref[...], zpad], axis=0).astype(BF16)
        col = lax.broadcasted_iota(jnp.int32, (1, LANES), 1)
        grow = col // ATT_HEADS
        tk = (grow - TOK0).astype(F32)
        ok = jnp.logical_and(jnp.logical_and(grow >= TOK0, grow < GROUP), tk <= tq)
        ok = jnp.logical_and(ok, (col % ATT_HEADS) == (rowi // GROUP))
        s = jnp.where(ok, _dot_nt(q, kb) - slope * (tq - tk), -jnp.inf)
        m_o = m_ref[...]
        m_n = jnp.maximum(m_o, jnp.max(s, axis=-1, keepdims=True))
        p = jnp.exp2(s - m_n)
        al = jnp.exp2(m_o - m_n)
        l_fin = al * l_ref[...] + jnp.sum(p, axis=-1, keepdims=True)
        acc_fin = (al * acc_ref[...] + _dot(p.astype(BF16), vb)) / l_fin

        lam = _lambda_full(lam_ref)
        rows8 = lax.broadcasted_iota(jnp.int32, (GROUP, 1), 0)
        for h in range(ATT_HEADS):
            blk = acc_fin[h * GROUP:(h + 1) * GROUP, :]
            o = blk - lam * pltpu.roll(blk, TOK0, axis=0)
            o = _rms(o, sub_ref[...]) * (1.0 - LAMBDA_INIT)
            o_ref[:, h * ATT_V_DIM:(h + 1) * ATT_V_DIM] = jnp.where(rows8 >= TOK0, o, 0.0).astype(o_ref.dtype)


def attn_sample(page_table, cache_k, cache_v, qh, k_new, v_new, lam4, subln, npp):
    nb, n_pages = page_table.shape
    n_pool = cache_k.shape[0]
    dk = ATT_HEADS * ATT_V_DIM
    ncol = PAGE_SIZE * ATT_HEADS
    ck = cache_k.reshape(n_pool, ncol, 2 * ATT_HEAD_DIM)
    cv = cache_v.reshape(n_pool, ncol, ATT_V_DIM)
    nr = qh.shape[1]
    nn = GROUP * ATT_HEADS
    slopes = jnp.asarray(np.tile(np.repeat(_alibi_slopes(), GROUP)[:, None], (1, LANES)))

    def page_spec(r):
        return pl.BlockSpec((None, ncol, LANES), lambda b, j, pt: (pt[b, j * npp + r], 0, 0))

    in_specs = [page_spec(r) for r in range(npp)] + [page_spec(r) for r in range(npp)]
    in_specs += [pl.BlockSpec((None, nr, LANES), lambda b, j, pt: (b, 0, 0)),
                 pl.BlockSpec((nn, LANES), lambda b, j, pt: (b, 0)),
                 pl.BlockSpec((nn, LANES), lambda b, j, pt: (b, 0)),
                 pl.BlockSpec(slopes.shape, lambda b, j, pt: (0, 0)),
                 pl.BlockSpec(lam4.shape, lambda b, j, pt: (0, 0)),
                 pl.BlockSpec(subln.shape, lambda b, j, pt: (0, 0))]
    grid_spec = pltpu.PrefetchScalarGridSpec(
        num_scalar_prefetch=1,
        grid=(nb, n_pages // npp),
        in_specs=in_specs,
        out_specs=pl.BlockSpec((GROUP, dk), lambda b, j, pt: (b, 0)),
        scratch_shapes=[pltpu.VMEM((nr, ncol), F32), pltpu.VMEM((nr, 1), F32), pltpu.VMEM((nr, 1), F32),
                        pltpu.VMEM((nr, LANES), F32)],
    )
    return pl.pallas_call(
        functools.partial(_attn_sample_kernel, npp=npp, past=float(n_pages * PAGE_SIZE)),
        grid_spec=grid_spec,
        out_shape=jax.ShapeDtypeStruct((nb * GROUP, dk), BF16),
        compiler_params=_cparams(("parallel", "arbitrary")),
        name="attn_sample",
    )(page_table, *([ck] * npp), *([cv] * npp), qh, k_new.reshape(nb * nn, LANES), v_new.reshape(nb * nn, LANES),
      slopes, lam4, subln)


def _block_diag4(w):
    nb = w.shape[0] // 4
    eye = jnp.eye(4, dtype=w.dtype)
    w4 = w.reshape(nb, 4, RNN_BW, RNN_BW)
    return jnp.einsum('cajk,ab->cajbk', w4, eye).reshape(nb, 4 * RNN_BW, 4 * RNN_BW).astype(BF16)


def _prep_weights(norm_mix, norm_ffn, norm_final, w_in0, conv_rnn_w, conv_rnn_b, rg_w_a, rg_b_a, rg_w_x, rg_b_x,
                  rg_lambda, conv_ssm_w, conv_ssm_b, dt_bias, a_log, d_skip, ssm_norm, w_out0, w_in1, lambda_q1,
                  lambda_k1, lambda_q2, lambda_k2, subln, w_out1, w_up, ffn_conv_w, ffn_conv_b, w_down):
    row = lambda v: v.reshape(1, -1)
    pad_l = lambda v: jnp.pad(v.reshape(1, -1), ((0, 0), (0, LANES - v.shape[-1])))
    return dict(
        norm_mix=[row(norm_mix[l]) for l in range(2)],
        norm_ffn=[row(norm_ffn[l]) for l in range(2)],
        norm_final=row(norm_final),
        w_in0=jnp.pad(w_in0, ((0, 0), (0, D_IN0_PAD - D_IN0))).astype(BF16),
        conv_rnn_w=conv_rnn_w, conv_rnn_b=row(conv_rnn_b),
        wa=_block_diag4(rg_w_a), ba=row(rg_b_a), wx=_block_diag4(rg_w_x), bx=row(rg_b_x), lam=row(rg_lambda),
        conv_ssm_w=conv_ssm_w, conv_ssm_b=row(conv_ssm_b),
        dt_bias=pad_l(dt_bias), a_log=pad_l(a_log),
        d_skip=row(jnp.repeat(d_skip, SSM_HEAD_DIM)), ssm_norm=row(ssm_norm),
        w_out0a=w_out0[:D_RNN].astype(BF16), w_out0b=w_out0[D_RNN:].astype(BF16),
        w_in1=w_in1.astype(BF16),
        lam4=jnp.stack([lambda_q1, lambda_k1, lambda_q2, lambda_k2]), subln=row(subln),
        w_out1=w_out1.astype(BF16),
        w_up=[w_up[l].astype(BF16) for l in range(2)],
        ffn_conv_w=[ffn_conv_w[l] for l in range(2)], ffn_conv_b=[row(ffn_conv_b[l]) for l in range(2)],
        w_down=[w_down[l].astype(BF16) for l in range(2)],
    )


def _state_t(s):
    b = s.shape[0]
    return jnp.transpose(s, (0, 3, 1, 2)).reshape(b, SSM_STATE, D_SSM)


def _state_untranspose(st):
    b = st.shape[0]
    return jnp.transpose(st.reshape(b, SSM_STATE, SSM_HEADS, SSM_HEAD_DIM), (0, 2, 3, 1))


def _prompt_trunk(x_prompt, w):
    bsz, seq, d = x_prompt.shape
    m = bsz * seq
    x = x_prompt.reshape(m, d)
    tm = min(512, seq)
    tr = min(256, seq)

    p0 = proj0(x, w['norm_mix'][0], w['w_in0'], tm)
    rnn_out, h_tail = rglru(p0, bsz, seq, tr, w['conv_rnn_w'], w['conv_rnn_b'], w['wa'], w['ba'], w['wx'],
                            w['bx'], w['lam'])
    lc = SSD_CHUNK
    nch = seq // lc
    hb = lc // HALO
    xbc_cb = (2 * D_RNN + D_SSM) // SSM_CONV_DIM
    halo_spec = pl.BlockSpec(
        (HALO, SSM_CONV_DIM), lambda b, i: (jnp.maximum((b * nch + i) * hb - 1, 0), xbc_cb))
    y_ssm, st = ssd(p0, xbc_cb, p0, halo_spec, p0, 2 * D_RNN // D_SSM, p0, (D_IN0_PAD - LANES) // LANES,
                    bsz, nch, lc, lc, w['conv_ssm_w'], w['conv_ssm_b'], w['dt_bias'], w['a_log'],
                    w['d_skip'], w['ssm_norm'])
    x = out_proj([rnn_out, y_ssm], [w['w_out0a'], w['w_out0b']], x, tm)
    x, u0 = ffn(x, bsz, seq, tm, w['norm_ffn'][0], w['w_up'][0], w['ffn_conv_w'][0], w['ffn_conv_b'][0],
                w['w_down'][0], w['norm_final'], False)

    k, v, qkv = proj1(x, w['norm_mix'][1], w['w_in1'], tm)
    o = attn_prompt(qkv, bsz, seq, min(256, seq), w['lam4'], w['subln'])
    x = out_proj([o], [w['w_out1']], x, tm)
    y, u1 = ffn(x, bsz, seq, tm, w['norm_ffn'][1], w['w_up'][1], w['ffn_conv_w'][1], w['ffn_conv_b'][1],
                w['w_down'][1], w['norm_final'], True)

    p3 = p0.reshape(bsz, seq, D_IN0_PAD)
    rnn_conv = p3[:, seq - (CONV_W - 1):, 0:D_RNN]
    ssm_conv = p3[:, seq - (CONV_W - 1):, 2 * D_RNN + D_SSM:2 * D_RNN + D_SSM + SSM_CONV_DIM]
    ffn_conv = jnp.stack([u0[:, SUBLANES - (FFN_CONV_W - 1):], u1[:, SUBLANES - (FFN_CONV_W - 1):]])
    return (y.reshape(bsz, seq, d), rnn_conv, h_tail[:, SUBLANES - 1], ssm_conv, _state_untranspose(st),
            k.reshape(bsz, seq, ATT_HEADS, 2 * ATT_HEAD_DIM), v.reshape(bsz, seq, ATT_HEADS, ATT_V_DIM), ffn_conv)


def _to_groups(x_tok, hist=None):
    b, t, c = x_tok.shape
    if hist is None:
        lead = jnp.zeros((b, TOK0, c), x_tok.dtype)
    else:
        k = hist.shape[1]
        lead = jnp.concatenate([jnp.zeros((b, TOK0 - k, c), x_tok.dtype), hist], axis=1)
    return jnp.concatenate([lead, x_tok], axis=1).reshape(b * GROUP, c)


def _sample_trunk(x_sample, state_rnn_conv, state_rnn_h, state_ssm_conv, state_ssm, cache_k, cache_v,
                  state_ffn_conv, page_table, w):
    nb, t, d = x_sample.shape
    assert t == GROUP - TOK0
    m = nb * GROUP
    x = _to_groups(x_sample)

    p0 = proj0(x, w['norm_mix'][0], w['w_in0'], m)
    p0g = p0.reshape(nb, GROUP, D_IN0_PAD)
    c_xbc = 2 * D_RNN + D_SSM
    xr = _to_groups(p0g[:, TOK0:, 0:D_RNN], state_rnn_conv)
    p0r = jnp.concatenate([xr, p0[:, D_RNN:2 * D_RNN]], axis=1)
    hinj = _to_groups(jnp.zeros((nb, t, D_RNN), F32), state_rnn_h[:, None, :])
    rnn_out, h_all = rglru(p0r, 1, m, m, w['conv_rnn_w'], w['conv_rnn_b'], w['wa'], w['ba'], w['wx'], w['bx'],
                           w['lam'], hinj=hinj)
    lc = SSD_CHUNK
    padc = lambda a: jnp.pad(a, ((0, 0), (0, lc - t), (0, 0))).reshape(nb * lc, a.shape[-1])
    xbc_p = padc(p0g[:, TOK0:, c_xbc:c_xbc + SSM_CONV_DIM])
    z_p = padc(p0g[:, TOK0:, 2 * D_RNN:2 * D_RNN + D_SSM])
    dt_p = padc(p0g[:, TOK0:, D_IN0_PAD - LANES:])
    halo = jnp.concatenate([jnp.zeros((nb, HALO - (CONV_W - 1), SSM_CONV_DIM), F32), state_ssm_conv], axis=1)
    halo_spec = pl.BlockSpec((None, HALO, SSM_CONV_DIM), lambda b, i: (b, 0, 0))
    y_p, st = ssd(xbc_p, 0, halo, halo_spec, z_p, 0, dt_p, 0, nb, 1, lc, t, w['conv_ssm_w'], w['conv_ssm_b'],
                  w['dt_bias'], w['a_log'], w['d_skip'], w['ssm_norm'], s0=_state_t(state_ssm))
    y_ssm = _to_groups(y_p.reshape(nb, lc, D_SSM)[:, 0:t])
    x = out_proj([rnn_out, y_ssm], [w['w_out0a'], w['w_out0b']], x, m)
    uinj0 = _to_groups(jnp.zeros((nb, t, 2 * D_FF), F32), state_ffn_conv[0])
    x, u0 = ffn(x, 1, m, m, w['norm_ffn'][0], w['w_up'][0], w['ffn_conv_w'][0], w['ffn_conv_b'][0],
                w['w_down'][0], w['norm_final'], False, uinj=uinj0)

    k, v, qkv = proj1(x, w['norm_mix'][1], w['w_in1'], m)
    qg = qkv[:, 0:D_MODEL].reshape(nb, GROUP, ATT_HEADS, 2, ATT_HEAD_DIM)[:, TOK0:]
    flip = jnp.array([[0, 1], [1, 0]], dtype=BF16)
    qh = jnp.einsum('bqhmd,nm->bhnqmd', qg, flip).reshape(nb, ATT_HEADS * GROUP, 2 * ATT_HEAD_DIM)
    o = attn_sample(page_table, cache_k, cache_v, qh, k, v, w['lam4'], w['subln'], npp=8)
    x = out_proj([o], [w['w_out1']], x, m)
    uinj1 = _to_groups(jnp.zeros((nb, t, 2 * D_FF), F32), state_ffn_conv[1])
    y, u1 = ffn(x, 1, m, m, w['norm_ffn'][1], w['w_up'][1], w['ffn_conv_w'][1], w['ffn_conv_b'][1],
                w['w_down'][1], w['norm_final'], True, uinj=uinj1)

    tok = lambda a: a.reshape(nb, GROUP, a.shape[-1])[:, TOK0:]
    rnn_conv = p0g[:, GROUP - (CONV_W - 1):, 0:D_RNN]
    ssm_conv = p0g[:, GROUP - (CONV_W - 1):, c_xbc:c_xbc + SSM_CONV_DIM]
    ffn_conv = jnp.stack([u0.reshape(nb, GROUP, -1)[:, GROUP - (FFN_CONV_W - 1):],
                          u1.reshape(nb, GROUP, -1)[:, GROUP - (FFN_CONV_W - 1):]])
    return (tok(y), rnn_conv, h_all.reshape(nb, GROUP, D_RNN)[:, GROUP - 1], ssm_conv, _state_untranspose(st),
            tok(k).reshape(nb, t, ATT_HEADS, 2 * ATT_HEAD_DIM), tok(v).reshape(nb, t, ATT_HEADS, ATT_V_DIM),
            ffn_conv)


def kernel(x_prompt, x_sample, state_rnn_conv, state_rnn_h, state_ssm_conv, state_ssm, cache_k, cache_v, state_ffn_conv, page_table, norm_mix, norm_ffn, norm_final, w_in0, conv_rnn_w, conv_rnn_b, rg_w_a, rg_b_a, rg_w_x, rg_b_x, rg_lambda, conv_ssm_w, conv_ssm_b, dt_bias, a_log, d_skip, ssm_norm, w_out0, w_in1, lambda_q1, lambda_k1, lambda_q2, lambda_k2, subln, w_out1, w_up, ffn_conv_w, ffn_conv_b, w_down):
    w = _prep_weights(norm_mix, norm_ffn, norm_final, w_in0, conv_rnn_w, conv_rnn_b, rg_w_a, rg_b_a, rg_w_x,
                      rg_b_x, rg_lambda, conv_ssm_w, conv_ssm_b, dt_bias, a_log, d_skip, ssm_norm, w_out0, w_in1,
                      lambda_q1, lambda_k1, lambda_q2, lambda_k2, subln, w_out1, w_up, ffn_conv_w, ffn_conv_b,
                      w_down)
    p = _prompt_trunk(x_prompt, w)
    s = _sample_trunk(x_sample, state_rnn_conv, state_rnn_h, state_ssm_conv, state_ssm, cache_k, cache_v,
                      state_ffn_conv, page_table, w)
    return (p[0], s[0]) + p[1:] + s[1:]
```

```python
import functools
import math

import jax
import jax.numpy as jnp
import numpy as np
from jax import lax
from jax.experimental import pallas as pl
from jax.experimental.pallas import tpu as pltpu

F32 = jnp.float32
BF16 = jnp.bfloat16

D_MODEL = 1024
D_RNN = 1024
RNN_BW = 64
RG_C = 8.0
CONV_W = 4
SSM_HEADS = 16
SSM_HEAD_DIM = 64
D_SSM = SSM_HEADS * SSM_HEAD_DIM
SSM_GROUPS = 2
SSM_STATE = 128
SSM_CONV_DIM = D_SSM + 2 * SSM_GROUPS * SSM_STATE
SSD_CHUNK = 128
D_IN0 = 2 * D_RNN + D_SSM + SSM_CONV_DIM + SSM_HEADS
D_IN0_PAD = 2 * D_RNN + D_SSM + SSM_CONV_DIM + 128
ATT_HEADS = 8
ATT_HEAD_DIM = 64
ATT_V_DIM = 128
LAMBDA_INIT = 0.8 - 0.6 * math.exp(-0.3 * 1)
D_FF = 3 * D_MODEL
FFN_CONV_W = 3
EPS = 1e-6
LOG2E = math.log2(math.e)
PAGE_SIZE = 128

SUBLANES = 8
BF16_ROWS = 16
LANES = 128
HALO = SUBLANES
GROUP = 8
TOK0 = 4
VMEM_LIMIT = 56 * 1024 * 1024


def _cparams(sem):
    return pltpu.CompilerParams(dimension_semantics=sem, vmem_limit_bytes=VMEM_LIMIT)


def _resident(shape):
    nd = len(shape)
    return pl.BlockSpec(shape, lambda *_: (0,) * nd, pipeline_mode=pl.Buffered(1))


def _rms(x, g):
    var = jnp.mean(x * x, axis=-1, keepdims=True)
    return x * lax.rsqrt(var + EPS) * g


def _softplus(x):
    return jnp.maximum(x, 0.0) + jnp.log1p(jnp.exp(-jnp.abs(x)))


def _dot(a, b):
    return jnp.dot(a, b, preferred_element_type=F32)


def _dot_nt(a, b):
    return lax.dot_general(a, b, (((1,), (1,)), ((), ())), preferred_element_type=F32)


def _proj0_kernel(x_ref, g_ref, w_ref, o_ref, *, tn):
    xn = _rms(x_ref[...], g_ref[...]).astype(BF16)
    n = w_ref.shape[1]
    for c in range(0, n, tn):
        w = min(tn, n - c)
        o_ref[:, c:c + w] = _dot(xn, w_ref[:, c:c + w])


def proj0(x, g, w, tm):
    m, d = x.shape
    n = w.shape[1]
    return pl.pallas_call(
        functools.partial(_proj0_kernel, tn=512),
        grid=(m // tm,),
        in_specs=[pl.BlockSpec((tm, d), lambda i: (i, 0)), _resident((1, d)), _resident((d, n))],
        out_specs=pl.BlockSpec((tm, n), lambda i: (i, 0)),
        out_shape=jax.ShapeDtypeStruct((m, n), F32),
        compiler_params=_cparams(("parallel",)),
        name="proj0",
    )(x, g, w)


def _proj1_kernel(x_ref, g_ref, w_ref, k_ref, v_ref, qkv_ref, *, tn):
    xn = _rms(x_ref[...], g_ref[...]).astype(BF16)
    d = k_ref.shape[1]
    for c in range(0, 3 * d, tn):
        r = _dot(xn, w_ref[:, c:c + tn])
        if c < d:
            qkv_ref[:, c:c + tn] = (r * (ATT_HEAD_DIM ** -0.5 * LOG2E)).astype(BF16)
        else:
            qkv_ref[:, c:c + tn] = r.astype(BF16)
            if c < 2 * d:
                k_ref[:, c - d:c - d + tn] = r
            else:
                v_ref[:, c - 2 * d:c - 2 * d + tn] = r


def proj1(x, g, w, tm):
    m, d = x.shape
    n = w.shape[1]
    return pl.pallas_call(
        functools.partial(_proj1_kernel, tn=512),
        grid=(m // tm,),
        in_specs=[pl.BlockSpec((tm, d), lambda i: (i, 0)), _resident((1, d)), _resident((d, n))],
        out_specs=[pl.BlockSpec((tm, d), lambda i: (i, 0)), pl.BlockSpec((tm, d), lambda i: (i, 0)),
                   pl.BlockSpec((tm, n), lambda i: (i, 0))],
        out_shape=[jax.ShapeDtypeStruct((m, d), F32), jax.ShapeDtypeStruct((m, d), F32),
                   jax.ShapeDtypeStruct((m, n), BF16)],
        compiler_params=_cparams(("parallel",)),
        name="proj1",
    )(x, g, w)


def _out_proj_kernel(*refs, n_in, tn):
    a_refs = refs[:n_in]
    w_refs = refs[n_in:2 * n_in]
    res_ref = refs[2 * n_in]
    o_ref = refs[2 * n_in + 1]
    n = o_ref.shape[1]
    for c in range(0, n, tn):
        acc = res_ref[:, c:c + tn]
        for a_ref, w_ref in zip(a_refs, w_refs):
            acc = acc + _dot(a_ref[...], w_ref[:, c:c + tn])
        o_ref[:, c:c + tn] = acc


def out_proj(a_list, w_list, res, tm):
    m, n = res.shape
    n_in = len(a_list)
    in_specs = [pl.BlockSpec((tm, a.shape[1]), lambda i: (i, 0)) for a in a_list]
    in_specs += [_resident(w.shape) for w in w_list]
    in_specs += [pl.BlockSpec((tm, n), lambda i: (i, 0))]
    return pl.pallas_call(
        functools.partial(_out_proj_kernel, n_in=n_in, tn=512),
        grid=(m // tm,),
        in_specs=in_specs,
        out_specs=pl.BlockSpec((tm, n), lambda i: (i, 0)),
        out_shape=jax.ShapeDtypeStruct((m, n), F32),
        compiler_params=_cparams(("parallel",)),
        name="out_proj",
    )(*a_list, *w_list, res)


def _conv_from_ext(ext_ref, w_ref, b_ref, width, tm, c0, c1):
    y = b_ref[:, c0:c1]
    for j in range(width):
        off = HALO - (width - 1) + j
        y = y + ext_ref[off:off + tm, c0:c1] * w_ref[j:j + 1, c0:c1]
    return y


def _scan_rows(a, b, c):
    tm = a.shape[0]
    rowm = lax.broadcasted_iota(jnp.int32, a.shape, 0) % SUBLANES
    s = 1
    while s < SUBLANES:
        keep = rowm >= s
        a_s = jnp.where(keep, pltpu.roll(a, s, axis=0), 1.0)
        b_s = jnp.where(keep, pltpu.roll(b, s, axis=0), 0.0)
        b = a * b_s + b
        a = a * a_s
        s *= 2
    hs = []
    for g in range(tm // SUBLANES):
        hg = a[g * SUBLANES:(g + 1) * SUBLANES] * c + b[g * SUBLANES:(g + 1) * SUBLANES]
        c = hg[SUBLANES - 1:SUBLANES, :]
        hs.append(hg)
    return jnp.concatenate(hs, axis=0)


def _rglru_kernel(*refs, tm, sample):
    if sample:
        (x_ref, y_ref, cw_ref, cb_ref, wa_ref, ba_ref, wx_ref, bx_ref, lam_ref, hinj_ref,
         out_ref, h_out_ref, ext_ref, carry_ref) = refs
    else:
        (x_ref, y_ref, halo_ref, cw_ref, cb_ref, wa_ref, ba_ref, wx_ref, bx_ref, lam_ref,
         out_ref, h_out_ref, ext_ref, carry_ref) = refs
    i = pl.program_id(1)
    d = x_ref.shape[1]

    @pl.when(i == 0)
    def _():
        carry_ref[...] = jnp.zeros_like(carry_ref)

    if sample:
        ext_ref[0:HALO, :] = jnp.zeros((HALO, d), F32)
    else:
        ext_ref[0:HALO, :] = jnp.where(i == 0, 0.0, halo_ref[...])
    ext_ref[HALO:HALO + tm, :] = x_ref[...]

    row = lax.broadcasted_iota(jnp.int32, (tm, 1), 0)
    tail = h_out_ref.shape[0]
    sp = _softplus(-lam_ref[...])
    cbw = 4 * RNN_BW
    for c in range(d // cbw):
        c0, c1 = c * cbw, (c + 1) * cbw
        xr = _conv_from_ext(ext_ref, cw_ref, cb_ref, CONV_W, tm, c0, c1)
        xb = xr.astype(BF16)
        r = jax.nn.sigmoid(_dot(xb, wa_ref[c]) + ba_ref[:, c0:c1])
        ig = jax.nn.sigmoid(_dot(xb, wx_ref[c]) + bx_ref[:, c0:c1])
        log_a = -RG_C * r * sp[:, c0:c1]
        a = jnp.exp(log_a)
        mult = jnp.sqrt(jnp.maximum(1.0 - a * a, 0.0))
        if sample:
            inj = (row % GROUP) == (TOK0 - 1)
            a = jnp.where(inj, 0.0, a)
            bv = jnp.where(inj, hinj_ref[:, c0:c1], xr * ig * mult)
        else:
            first = jnp.logical_and(i == 0, row == 0)
            a = jnp.where(first, 0.0, a)
            mult = jnp.where(first, 1.0, mult)
            bv = xr * ig * mult
        for k in range(cbw // LANES):
            l0, l1 = c0 + k * LANES, c0 + (k + 1) * LANES
            h = _scan_rows(a[:, k * LANES:(k + 1) * LANES], bv[:, k * LANES:(k + 1) * LANES], carry_ref[:, l0:l1])
            carry_ref[:, l0:l1] = h[tm - 1:tm, :]
            out_ref[:, l0:l1] = (h * jax.nn.gelu(y_ref[:, l0:l1])).astype(out_ref.dtype)
            h_out_ref[:, l0:l1] = h[tm - tail:tm, :]


def rglru(proj, nseq, seq, tm, cw, cb, wa, ba, wx, bx, lam, hinj=None):
    sample = hinj is not None
    nt = seq // tm
    d = D_RNN
    xmap = lambda b, i: (b * nt + i, 0)
    in_specs = [pl.BlockSpec((tm, d), xmap), pl.BlockSpec((tm, d), lambda b, i: (b * nt + i, 1))]
    args = [proj, proj]
    if not sample:
        hb = tm // HALO
        in_specs.append(pl.BlockSpec((HALO, d), lambda b, i: (jnp.maximum((b * nt + i) * hb - 1, 0), 0)))
        args.append(proj)
    in_specs += [_resident(cw.shape), _resident(cb.shape), _resident(wa.shape), _resident(ba.shape),
                 _resident(wx.shape), _resident(bx.shape), _resident(lam.shape)]
    args += [cw, cb, wa, ba, wx, bx, lam]
    if sample:
        in_specs.append(pl.BlockSpec((tm, d), xmap))
        args.append(hinj)
        h_spec = pl.BlockSpec((tm, d), xmap)
        h_shape = jax.ShapeDtypeStruct((nseq * seq, d), F32)
    else:
        h_spec = pl.BlockSpec((None, SUBLANES, d), lambda b, i: (b, 0, 0))
        h_shape = jax.ShapeDtypeStruct((nseq, SUBLANES, d), F32)
    return pl.pallas_call(
        functools.partial(_rglru_kernel, tm=tm, sample=sample),
        grid=(nseq, nt),
        in_specs=in_specs,
        out_specs=[pl.BlockSpec((tm, d), xmap), h_spec],
        out_shape=[jax.ShapeDtypeStruct((nseq * seq, d), BF16), h_shape],
        scratch_shapes=[pltpu.VMEM((tm + HALO, d), F32), pltpu.VMEM((1, d), F32)],
        compiler_params=_cparams(("parallel", "arbitrary")),
        name="rglru",
    )(*args)


def _pair(v, hp, lane_lo):
    return jnp.where(lane_lo, v[:, 2 * hp:2 * hp + 1], v[:, 2 * hp + 1:2 * hp + 2])


def _ssd_kernel(*refs, lc, n_valid, has_init):
    if has_init:
        (xbc_ref, halo_ref, z_ref, dt_ref, cw_ref, cb_ref, dtb_ref, alog_ref, dsk_ref, nrm_ref, s0_ref,
         y_ref, sout_ref, ext_ref, st_ref, yacc_ref) = refs
    else:
        (xbc_ref, halo_ref, z_ref, dt_ref, cw_ref, cb_ref, dtb_ref, alog_ref, dsk_ref, nrm_ref,
         y_ref, sout_ref, ext_ref, st_ref, yacc_ref) = refs
    i = pl.program_id(1)
    n = SSM_STATE
    hd2 = 2 * SSM_HEAD_DIM
    gw = D_SSM // SSM_GROUPS

    @pl.when(i == 0)
    def _():
        if has_init:
            st_ref[...] = s0_ref[...]
        else:
            st_ref[...] = jnp.zeros_like(st_ref)

    if has_init:
        ext_ref[0:HALO, :] = halo_ref[...]
    else:
        ext_ref[0:HALO, :] = jnp.where(i == 0, 0.0, halo_ref[...])
    ext_ref[HALO:HALO + lc, :] = xbc_ref[...]
    xbc = _conv_from_ext(ext_ref, cw_ref, cb_ref, CONV_W, lc, 0, SSM_CONV_DIM)
    xbc = xbc * jax.nn.sigmoid(xbc)
    xs = xbc[:, 0:D_SSM]

    row = lax.broadcasted_iota(jnp.int32, (lc, 1), 0)
    dt = _softplus(dt_ref[...] + dtb_ref[...])
    if n_valid < lc:
        dt = jnp.where(row < n_valid, dt, 0.0)
    acs = dt * (-jnp.exp(alog_ref[...]))
    s = 1
    while s < lc:
        acs = acs + jnp.where(row >= s, pltpu.roll(acs, s, axis=0), 0.0)
        s *= 2
    if lc == LANES:
        acs_t = acs.T
    else:
        acs_t = jnp.concatenate([acs, jnp.zeros((LANES - lc, LANES), F32)], axis=0).T[:, 0:lc]
    a_last = acs[lc - 1:lc, :]
    ea = jnp.exp(acs)
    dsv = jnp.exp(a_last - acs)
    cdec = jnp.exp(a_last)

    lane_lo = lax.broadcasted_iota(jnp.int32, (1, hd2), 1) < SSM_HEAD_DIM
    causal = (lax.broadcasted_iota(jnp.int32, (lc, lc), 0) >= lax.broadcasted_iota(jnp.int32, (lc, lc), 1))
    for g in range(SSM_GROUPS):
        bm = xbc[:, D_SSM + g * n:D_SSM + (g + 1) * n]
        cm = xbc[:, D_SSM + SSM_GROUPS * n + g * n:D_SSM + SSM_GROUPS * n + (g + 1) * n]
        bm_b = bm.astype(BF16)
        cm_b = cm.astype(BF16)
        cbm = _dot_nt(cm_b, bm_b)
        st_old = st_ref[:, g * gw:(g + 1) * gw]
        yoff = _dot(cm_b, st_old.astype(BF16))
        if lc == LANES:
            bm_t = bm.T.astype(BF16)
        else:
            bm_t = jnp.concatenate([bm, jnp.zeros((LANES - lc, n), F32)], axis=0).T[:, 0:lc].astype(BF16)
        w_parts = []
        cd_parts = []
        for pp in range(gw // hd2):
            hp = g * (gw // hd2) + pp
            h0, h1 = 2 * hp, 2 * hp + 1
            c0 = hp * hd2
            xg = xs[:, c0:c0 + hd2] * _pair(dt, hp, lane_lo)
            xg_b = xg.astype(BF16)
            yd = []
            for h in (h0, h1):
                seg = acs[:, h:h + 1] - acs_t[h:h + 1, :]
                dec = jnp.exp(jnp.where(causal, seg, -jnp.inf))
                yd.append(_dot((cbm * dec).astype(BF16), xg_b))
            ydiag = jnp.where(lane_lo, yd[0], yd[1])
            yo = yoff[:, pp * hd2:(pp + 1) * hd2] * _pair(ea, hp, lane_lo)
            yacc_ref[:, c0:c0 + hd2] = ydiag + yo + dsk_ref[:, c0:c0 + hd2] * xs[:, c0:c0 + hd2]
            w_parts.append((xg * _pair(dsv, hp, lane_lo)).astype(BF16))
            cd_parts.append(_pair(cdec, hp, lane_lo))
        w_all = jnp.concatenate(w_parts, axis=1)
        cd_all = jnp.concatenate(cd_parts, axis=1)
        st_ref[:, g * gw:(g + 1) * gw] = cd_all * st_old + _dot(bm_t, w_all)

    z = z_ref[...]
    y = yacc_ref[...] * (z * jax.nn.sigmoid(z))
    y_ref[...] = _rms(y, nrm_ref[...]).astype(y_ref.dtype)
    sout_ref[...] = st_ref[...]


def ssd(xbc_arr, xbc_col, halo_arr, halo_map, z_arr, z_col, dt_arr, dt_col, nseq, nchunk, lc, n_valid,
        cw, cb, dtb, alog, dsk, nrm, s0=None):
    has_init = s0 is not None
    rmap = lambda col: (lambda b, i: (b * nchunk + i, col))
    in_specs = [pl.BlockSpec((lc, SSM_CONV_DIM), rmap(xbc_col)),
                halo_map,
                pl.BlockSpec((lc, D_SSM), rmap(z_col)),
                pl.BlockSpec((lc, LANES), rmap(dt_col)),
                _resident(cw.shape), _resident(cb.shape), _resident(dtb.shape), _resident(alog.shape),
                _resident(dsk.shape), _resident(nrm.shape)]
    args = [xbc_arr, halo_arr, z_arr, dt_arr, cw, cb, dtb, alog, dsk, nrm]
    if has_init:
        in_specs.append(pl.BlockSpec((None, SSM_STATE, D_SSM), lambda b, i: (b, 0, 0)))
        args.append(s0)
    return pl.pallas_call(
        functools.partial(_ssd_kernel, lc=lc, n_valid=n_valid, has_init=has_init),
        grid=(nseq, nchunk),
        in_specs=in_specs,
        out_specs=[pl.BlockSpec((lc, D_SSM), lambda b, i: (b * nchunk + i, 0)),
                   pl.BlockSpec((None, SSM_STATE, D_SSM), lambda b, i: (b, 0, 0))],
        out_shape=[jax.ShapeDtypeStruct((nseq * nchunk * lc, D_SSM), BF16),
                   jax.ShapeDtypeStruct((nseq, SSM_STATE, D_SSM), F32)],
        scratch_shapes=[pltpu.VMEM((lc + HALO, SSM_CONV_DIM), F32), pltpu.VMEM((SSM_STATE, D_SSM), F32),
                        pltpu.VMEM((lc, D_SSM), F32)],
        compiler_params=_cparams(("parallel", "arbitrary")),
        name="ssd",
    )(*args)


def _ffn_kernel(*refs, tm, fc, sample, final_norm):
    if sample:
        (x_ref, g_ref, wup_ref, cw_ref, cb_ref, wdn_ref, gf_ref, uinj_ref,
         o_ref, u_out_ref, ext_ref) = refs
    else:
        (x_ref, halo_ref, g_ref, wup_ref, cw_ref, cb_ref, wdn_ref, gf_ref,
         o_ref, u_out_ref, ext_ref) = refs
    i = pl.program_id(1)
    d_ff = wdn_ref.shape[0]
    x = x_ref[...]
    hn = _rms(x, g_ref[...]).astype(BF16)
    if sample:
        row = lax.broadcasted_iota(jnp.int32, (tm, 1), 0) % GROUP
        hist = jnp.logical_and(row >= TOK0 - (FFN_CONV_W - 1), row < TOK0)
    else:
        halo_n = _rms(halo_ref[...], g_ref[...]).astype(BF16)
    tail = u_out_ref.shape[0]

    def up(c):
        out = []
        for base in (c, d_ff + c):
            u = _dot(hn, wup_ref[:, base:base + fc])
            uh = None if sample else _dot(halo_n, wup_ref[:, base:base + fc])
            out.append((u, uh))
        return out

    acc = x
    ups = up(0)
    for c in range(0, d_ff, fc):
        ups_next = up(c + fc) if c + fc < d_ff else None
        conv = []
        for n, base in enumerate((c, d_ff + c)):
            u, uh = ups[n]
            e_ref = ext_ref.at[n]
            if sample:
                u = jnp.where(hist, uinj_ref[:, base:base + fc], u)
                e_ref[0:HALO, :] = jnp.zeros((HALO, fc), F32)
            else:
                e_ref[0:HALO, :] = jnp.where(i == 0, 0.0, uh)
            e_ref[HALO:HALO + tm, :] = u
            u_out_ref[:, base:base + fc] = e_ref[HALO + tm - tail:HALO + tm, :]
            y = cb_ref[:, base:base + fc]
            for j in range(FFN_CONV_W):
                off = HALO - (FFN_CONV_W - 1) + j
                y = y + e_ref[off:off + tm, :] * cw_ref[j:j + 1, base:base + fc]
            conv.append(y)
        act = (jax.nn.gelu(conv[0]) * conv[1]).astype(BF16)
        acc = acc + _dot(act, wdn_ref[c:c + fc, :])
        ups = ups_next
    if final_norm:
        acc = _rms(acc, gf_ref[...])
    o_ref[...] = acc


def ffn(x, nseq, seq, tm, g, wup, cw, cb, wdn, gf, final_norm, uinj=None):
    sample = uinj is not None
    nt = seq // tm
    d = x.shape[1]
    d_up = wup.shape[1]
    xmap = lambda b, i: (b * nt + i, 0)
    in_specs = [pl.BlockSpec((tm, d), xmap)]
    args = [x]
    if not sample:
        hb = tm // HALO
        in_specs.append(pl.BlockSpec((HALO, d), lambda b, i: (jnp.maximum((b * nt + i) * hb - 1, 0), 0)))
        args.append(x)
    in_specs += [_resident(g.shape), _resident(wup.shape), _resident(cw.shape), _resident(cb.shape),
                 _resident(wdn.shape), _resident(gf.shape)]
    args += [g, wup, cw, cb, wdn, gf]
    if sample:
        in_specs.append(pl.BlockSpec((tm, d_up), xmap))
        args.append(uinj)
        u_spec = pl.BlockSpec((tm, d_up), xmap)
        u_shape = jax.ShapeDtypeStruct((nseq * seq, d_up), F32)
    else:
        u_spec = pl.BlockSpec((None, SUBLANES, d_up), lambda b, i: (b, 0, 0))
        u_shape = jax.ShapeDtypeStruct((nseq, SUBLANES, d_up), F32)
    fc = 512
    return pl.pallas_call(
        functools.partial(_ffn_kernel, tm=tm, fc=fc, sample=sample, final_norm=final_norm),
        grid=(nseq, nt),
        in_specs=in_specs,
        out_specs=[pl.BlockSpec((tm, d), xmap), u_spec],
        out_shape=[jax.ShapeDtypeStruct((nseq * seq, d), F32), u_shape],
        scratch_shapes=[pltpu.VMEM((2, tm + HALO, fc), F32)],
        compiler_params=_cparams(("parallel", "arbitrary")),
        name="ffn",
    )(*args)


def _lambda_full(lam_ref):
    lq1 = lam_ref[0:1, :]
    lk1 = lam_ref[1:2, :]
    lq2 = lam_ref[2:3, :]
    lk2 = lam_ref[3:4, :]
    return (jnp.exp(jnp.sum(lq1 * lk1, axis=-1, keepdims=True))
            - jnp.exp(jnp.sum(lq2 * lk2, axis=-1, keepdims=True)) + LAMBDA_INIT)


def _alibi_slopes():
    return np.asarray(2.0 ** (-8.0 * np.arange(1, ATT_HEADS + 1) / ATT_HEADS), dtype=np.float32)


def _attn_prompt_kernel(q_ref, k_ref, v_ref, slope_ref, lam_ref, subt_ref, o_ref, vt_ref, bias_ref, s_ref, p_ref,
                        m_ref, acc_ref, *, t, tk, unroll):
    h = pl.program_id(1)
    i = pl.program_id(2)
    dh = ATT_HEAD_DIM
    hw = 2 * dh
    nkb = vt_ref.shape[0]
    ns = t // LANES
    nd = t // tk

    @pl.when(i == 0)
    def _():
        for jb in range(nkb):
            for c in range(tk // LANES):
                r0 = jb * tk + c * LANES
                vt_ref[jb, 0:hw, c * LANES:(c + 1) * LANES] = v_ref[r0:r0 + LANES, :].astype(F32).T.astype(BF16)
            ones_row = lax.broadcasted_iota(jnp.int32, (BF16_ROWS, tk), 0) == 0
            vt_ref[jb, hw:hw + BF16_ROWS, :] = jnp.where(ones_row, 1.0, 0.0).astype(BF16)

    slope = slope_ref[pl.ds(h, 1), :][:, 0:1]

    @pl.when(i == 0)
    def _():
        kk = lax.broadcasted_iota(jnp.int32, (tk, t), 0)
        qi = lax.broadcasted_iota(jnp.int32, (tk, t), 1)
        b = (kk - qi).astype(F32) * (slope * LOG2E)
        bias_ref[...] = jnp.concatenate([b, b], axis=1)

    q = q_ref[...]
    lane = lax.broadcasted_iota(jnp.int32, (1, 2 * dh), 1)
    zero = jnp.zeros_like(q)
    qq = jnp.concatenate([jnp.where(lane < dh, q, zero), jnp.where(lane >= dh, q, zero)], axis=0)

    m_ref[...] = jnp.full_like(m_ref, -jnp.inf)
    acc_ref[...] = jnp.zeros_like(acc_ref)

    def put_scores(j, slot):
        s_ref[slot] = _dot_nt(k_ref[pl.ds(pl.multiple_of(j * tk, tk), tk), :], qq) + bias_ref[...]

    def strip(slot, pslot, c, off, d):
        cb = c % ns
        sl = slice(c * LANES, (c + 1) * LANES)
        nk, masked = tk, False
        if d is not None:
            nk = min(tk, max(0, (cb + 1) * LANES - d * tk))
            masked = d * tk + nk > cb * LANES
        if nk == 0:
            p_ref[pslot, :, sl] = jnp.zeros((tk, LANES), BF16)
            return jnp.ones((1, LANES), F32)
        if masked:
            kidx = lax.broadcasted_iota(jnp.int32, (nk, LANES), 0) + d * tk
            qidx = lax.broadcasted_iota(jnp.int32, (nk, LANES), 1) + cb * LANES
            causal = kidx <= qidx
        s1 = s_ref[slot, 0:nk, sl]
        if masked:
            s1 = jnp.where(causal, s1, -jnp.inf)
        m_old = m_ref[:, sl]
        m_new = jnp.maximum(m_old, jnp.max(s1, axis=0, keepdims=True) + off)
        m_ref[:, sl] = m_new
        s2 = s_ref[slot, 0:nk, sl] - (m_new - off)
        if masked:
            s2 = jnp.where(causal, s2, -jnp.inf)
        p = jnp.exp2(s2)
        alpha = jnp.exp2(m_old - m_new)
        p_ref[pslot, 0:nk, sl] = p.astype(BF16)
        if nk < tk:
            p_ref[pslot, nk:tk, sl] = jnp.zeros((tk - nk, LANES), BF16)
        return alpha

    def run(js, kinds, ahead):
        order = list(js) + list(ahead)
        pending = None
        for n, (j, d) in enumerate(zip(js, kinds)):
            if n + 2 < len(order):
                put_scores(order[n + 2], (n + 2) % 4)
            off = (slope * LOG2E) * jnp.full((1, 1), j * tk - i * t, jnp.int32).astype(F32)
            alpha = jnp.concatenate([strip(n % 4, n % 4, c, off, d) for c in range(2 * ns)], axis=1)
            pv = _dot(vt_ref[j], p_ref[n % 4])
            if pending is not None:
                acc_ref[...] = pending[0] * acc_ref[...] + pending[1]
            pending = (alpha, pv)
        acc_ref[...] = pending[0] * acc_ref[...] + pending[1]

    nfull = i * nd
    ng = nfull // unroll

    def body(g, carry):
        base = g * unroll
        run([base + u for u in range(unroll)], [None] * unroll, [base + unroll, base + unroll + 1])
        return carry

    put_scores(0, 0)
    put_scores(1, 1)
    lax.fori_loop(0, ng, body, 0)
    for rem in range(0, unroll, nd):

        @pl.when(nfull - ng * unroll == rem)
        def _(rem=rem):
            run([ng * unroll + u for u in range(rem)] + [nfull + d for d in range(nd)],
                [None] * rem + list(range(nd)), [])

    lam = _lambda_full(lam_ref)
    o = acc_ref[0:hw, :] / acc_ref[hw:hw + 1, :]
    o = o[:, 0:t] - lam * o[:, t:2 * t]
    var = jnp.mean(o * o, axis=0, keepdims=True)
    o = o * lax.rsqrt(var + EPS) * subt_ref[:, 0:1] * (1.0 - LAMBDA_INIT)
    for c in range(ns):
        o_ref[c * LANES:(c + 1) * LANES, :] = o[:, c * LANES:(c + 1) * LANES].T.astype(o_ref.dtype)


def attn_prompt(qkv, nseq, seq, t, lam4, subln, tk=None):
    tk = t if tk is None else tk
    assert t % tk == 0 and 4 % (t // tk) == 0 and seq >= 2 * tk
    nq = seq // t
    hw = 2 * ATT_HEAD_DIM
    qkv3 = qkv.reshape(nseq, seq, qkv.shape[1])
    slopes = jnp.asarray(np.tile(_alibi_slopes()[:, None], (1, LANES)))
    subt = jnp.tile(subln.reshape(hw, 1), (1, LANES))
    return pl.pallas_call(
        functools.partial(_attn_prompt_kernel, t=t, tk=tk, unroll=4),
        grid=(nseq, ATT_HEADS, nq),
        in_specs=[pl.BlockSpec((None, t, hw), lambda b, h, i: (b, i, h)),
                  pl.BlockSpec((None, seq, hw), lambda b, h, i: (b, 0, ATT_HEADS + h)),
                  pl.BlockSpec((None, seq, hw), lambda b, h, i: (b, 0, 2 * ATT_HEADS + h)),
                  _resident(slopes.shape), _resident(lam4.shape), _resident(subt.shape)],
        out_specs=pl.BlockSpec((None, t, hw), lambda b, h, i: (b, i, h)),
        out_shape=jax.ShapeDtypeStruct((nseq, seq, ATT_HEADS * ATT_V_DIM), BF16),
        scratch_shapes=[pltpu.VMEM((seq // tk, hw + BF16_ROWS, tk), BF16), pltpu.VMEM((tk, 2 * t), F32),
                        pltpu.VMEM((4, tk, 2 * t), F32), pltpu.VMEM((4, tk, 2 * t), BF16),
                        pltpu.VMEM((1, 2 * t), F32), pltpu.VMEM((hw + BF16_ROWS, 2 * t), F32)],
        compiler_params=_cparams(("parallel", "parallel", "arbitrary")),
        name="attn_prompt",
    )(qkv3, qkv3, qkv3, slopes, lam4, subt).reshape(nseq * seq, ATT_HEADS * ATT_V_DIM)


def _attn_sample_kernel(pt_ref, *refs, npp, past):
    k_refs = refs[:npp]
    v_refs = refs[npp:2 * npp]
    (q_ref, knew_ref, vnew_ref, slope_ref, lam_ref, sub_ref, o_ref, b0_ref, m_ref, l_ref, acc_ref) = refs[2 * npp:]
    j = pl.program_id(1)
    nj = pl.num_programs(1)
    nr = q_ref.shape[0]
    ncol = PAGE_SIZE * ATT_HEADS
    rowi = lax.broadcasted_iota(jnp.int32, (nr, 1), 0)
    slope = slope_ref[:, 0:1] * LOG2E
    tq = (rowi % TOK0).astype(F32)

    @pl.when(j == 0)
    def _():
        m_ref[...] = jnp.full_like(m_ref, -jnp.inf)
        l_ref[...] = jnp.zeros_like(l_ref)
        acc_ref[...] = jnp.zeros_like(acc_ref)
        col = lax.broadcasted_iota(jnp.int32, (1, ncol), 1)
        same_head = (col % ATT_HEADS) == (rowi // GROUP)
        b0_ref[...] = jnp.where(same_head, slope * ((col // ATT_HEADS).astype(F32) - tq), -jnp.inf)

    q = q_ref[...]
    ss = [_dot_nt(q, k_refs[r][...].astype(BF16)) for r in range(npp)]
    stats = []
    for s in ss:
        s = s + b0_ref[...]
        mr = jnp.max(s, axis=-1, keepdims=True)
        p = jnp.exp2(s - mr)
        stats.append((mr, jnp.sum(p, axis=-1, keepdims=True), p.astype(BF16)))
    parts = []
    for r, (mr, lr, p) in enumerate(stats):
        pv = _dot(p, v_refs[r][...].astype(BF16))
        page0 = jnp.full((1, 1), (j * npp + r) * PAGE_SIZE, jnp.int32).astype(F32)
        parts.append((mr + slope * (page0 - past), lr, pv))
    m_old = m_ref[...]
    m_new = m_old
    for mt, _, _ in parts:
        m_new = jnp.maximum(m_new, mt)
    alpha = jnp.exp2(m_old - m_new)
    l_acc = alpha * l_ref[...]
    acc = alpha * acc_ref[...]
    for mt, lr, pv in parts:
        wgt = jnp.exp2(mt - m_new)
        l_acc = l_acc + wgt * lr
        acc = acc + wgt * pv
    m_ref[...] = m_new
    l_ref[...] = l_acc
    acc_ref[...] = acc

    @pl.when(j == nj - 1)
    def _():
        nn = knew_ref.shape[0]
        zpad = jnp.zeros((LANES - nn, LANES), F32)
        kb = jnp.concatenate([knew_ref[...], zpad], axis=0).astype(BF16)
        vb = jnp.concatenate([vnew_ref[...], zpad], axis=0).astype(BF16)
        col = lax.broadcasted_iota(jnp.int32, (1, LANES), 1)
        grow = col // ATT_HEADS
        tk = (grow - TOK0).astype(F32)
        ok = jnp.logical_and(jnp.logical_and(grow >= TOK0, grow < GROUP), tk <= tq)
        ok = jnp.logical_and(ok, (col % ATT_HEADS) == (rowi // GROUP))
        s = jnp.where(ok, _dot_nt(q, kb) - slope * (tq - tk), -jnp.inf)
        m_o = m_ref[...]
        m_n = jnp.maximum(m_o, jnp.max(s, axis=-1, keepdims=True))
        p = jnp.exp2(s - m_n)
        al = jnp.exp2(m_o - m_n)
        l_fin = al * l_ref[...] + jnp.sum(p, axis=-1, keepdims=True)
        acc_fin = (al * acc_ref[...] + _dot(p.astype(BF16), vb)) / l_fin

        lam = _lambda_full(lam_ref)
        rows8 = lax.broadcasted_iota(jnp.int32, (GROUP, 1), 0)
        for h in range(ATT_HEADS):
            blk = acc_fin[h * GROUP:(h + 1) * GROUP, :]
            o = blk - lam * pltpu.roll(blk, TOK0, axis=0)
            o = _rms(o, sub_ref[...]) * (1.0 - LAMBDA_INIT)
            o_ref[:, h * ATT_V_DIM:(h + 1) * ATT_V_DIM] = jnp.where(rows8 >= TOK0, o, 0.0).astype(o_ref.dtype)


def attn_sample(page_table, cache_k, cache_v, qh, k_new, v_new, lam4, subln, npp):
    nb, n_pages = page_table.shape
    n_pool = cache_k.shape[0]
    dk = ATT_HEADS * ATT_V_DIM
    ncol = PAGE_SIZE * ATT_HEADS
    ck = cache_k.reshape(n_pool, ncol, 2 * ATT_HEAD_DIM)
    cv = cache_v.reshape(n_pool, ncol, ATT_V_DIM)
    nr = qh.shape[1]
    nn = GROUP * ATT_HEADS
    slopes = jnp.asarray(np.tile(np.repeat(_alibi_slopes(), GROUP)[:, None], (1, LANES)))

    def page_spec(r):
        return pl.BlockSpec((None, ncol, LANES), lambda b, j, pt: (pt[b, j * npp + r], 0, 0))

    in_specs = [page_spec(r) for r in range(npp)] + [page_spec(r) for r in range(npp)]
    in_specs += [pl.BlockSpec((None, nr, LANES), lambda b, j, pt: (b, 0, 0)),
                 pl.BlockSpec((nn, LANES), lambda b, j, pt: (b, 0)),
                 pl.BlockSpec((nn, LANES), lambda b, j, pt: (b, 0)),
                 pl.BlockSpec(slopes.shape, lambda b, j, pt: (0, 0)),
                 pl.BlockSpec(lam4.shape, lambda b, j, pt: (0, 0)),
                 pl.BlockSpec(subln.shape, lambda b, j, pt: (0, 0))]
    grid_spec = pltpu.PrefetchScalarGridSpec(
        num_scalar_prefetch=1,
        grid=(nb, n_pages // npp),
        in_specs=in_specs,
        out_specs=pl.BlockSpec((GROUP, dk), lambda b, j, pt: (b, 0)),
        scratch_shapes=[pltpu.VMEM((nr, ncol), F32), pltpu.VMEM((nr, 1), F32), pltpu.VMEM((nr, 1), F32),
                        pltpu.VMEM((nr, LANES), F32)],
    )
    return pl.pallas_call(
        functools.partial(_attn_sample_kernel, npp=npp, past=float(n_pages * PAGE_SIZE)),
        grid_spec=grid_spec,
        out_shape=jax.ShapeDtypeStruct((nb * GROUP, dk), BF16),
        compiler_params=_cparams(("parallel", "arbitrary")),
        name="attn_sample",
    )(page_table, *([ck] * npp), *([cv] * npp), qh, k_new.reshape(nb * nn, LANES), v_new.reshape(nb * nn, LANES),
      slopes, lam4, subln)


def _block_diag4(w):
    nb = w.shape[0] // 4
    eye = jnp.eye(4, dtype=w.dtype)
    w4 = w.reshape(nb, 4, RNN_BW, RNN_BW)
    return jnp.einsum('cajk,ab->cajbk', w4, eye).reshape(nb, 4 * RNN_BW, 4 * RNN_BW).astype(BF16)


def _prep_weights(norm_mix, norm_ffn, norm_final, w_in0, conv_rnn_w, conv_rnn_b, rg_w_a, rg_b_a, rg_w_x, rg_b_x,
                  rg_lambda, conv_ssm_w, conv_ssm_b, dt_bias, a_log, d_skip, ssm_norm, w_out0, w_in1, lambda_q1,
                  lambda_k1, lambda_q2, lambda_k2, subln, w_out1, w_up, ffn_conv_w, ffn_conv_b, w_down):
    row = lambda v: v.reshape(1, -1)
    pad_l = lambda v: jnp.pad(v.reshape(1, -1), ((0, 0), (0, LANES - v.shape[-1])))
    return dict(
        norm_mix=[row(norm_mix[l]) for l in range(2)],
        norm_ffn=[row(norm_ffn[l]) for l in range(2)],
        norm_final=row(norm_final),
        w_in0=jnp.pad(w_in0, ((0, 0), (0, D_IN0_PAD - D_IN0))).astype(BF16),
        conv_rnn_w=conv_rnn_w, conv_rnn_b=row(conv_rnn_b),
        wa=_block_diag4(rg_w_a), ba=row(rg_b_a), wx=_block_diag4(rg_w_x), bx=row(rg_b_x), lam=row(rg_lambda),
        conv_ssm_w=conv_ssm_w, conv_ssm_b=row(conv_ssm_b),
        dt_bias=pad_l(dt_bias), a_log=pad_l(a_log),
        d_skip=row(jnp.repeat(d_skip, SSM_HEAD_DIM)), ssm_norm=row(ssm_norm),
        w_out0a=w_out0[:D_RNN].astype(BF16), w_out0b=w_out0[D_RNN:].astype(BF16),
        w_in1=w_in1.astype(BF16),
        lam4=jnp.stack([lambda_q1, lambda_k1, lambda_q2, lambda_k2]), subln=row(subln),
        w_out1=w_out1.astype(BF16),
        w_up=[w_up[l].astype(BF16) for l in range(2)],
        ffn_conv_w=[ffn_conv_w[l] for l in range(2)], ffn_conv_b=[row(ffn_conv_b[l]) for l in range(2)],
        w_down=[w_down[l].astype(BF16) for l in range(2)],
    )


def _state_t(s):
    b = s.shape[0]
    return jnp.transpose(s, (0, 3, 1, 2)).reshape(b, SSM_STATE, D_SSM)


def _state_untranspose(st):
    b = st.shape[0]
    return jnp.transpose(st.reshape(b, SSM_STATE, SSM_HEADS, SSM_HEAD_DIM), (0, 2, 3, 1))


def _prompt_trunk(x_prompt, w):
    bsz, seq, d = x_prompt.shape
    m = bsz * seq
    x = x_prompt.reshape(m, d)
    tm = min(512, seq)
    tr = min(256, seq)

    p0 = proj0(x, w['norm_mix'][0], w['w_in0'], tm)
    rnn_out, h_tail = rglru(p0, bsz, seq, tr, w['conv_rnn_w'], w['conv_rnn_b'], w['wa'], w['ba'], w['wx'],
                            w['bx'], w['lam'])
    lc = SSD_CHUNK
    nch = seq // lc
    hb = lc // HALO
    xbc_cb = (2 * D_RNN + D_SSM) // SSM_CONV_DIM
    halo_spec = pl.BlockSpec(
        (HALO, SSM_CONV_DIM), lambda b, i: (jnp.maximum((b * nch + i) * hb - 1, 0), xbc_cb))
    y_ssm, st = ssd(p0, xbc_cb, p0, halo_spec, p0, 2 * D_RNN // D_SSM, p0, (D_IN0_PAD - LANES) // LANES,
                    bsz, nch, lc, lc, w['conv_ssm_w'], w['conv_ssm_b'], w['dt_bias'], w['a_log'],
                    w['d_skip'], w['ssm_norm'])
    x = out_proj([rnn_out, y_ssm], [w['w_out0a'], w['w_out0b']], x, tm)
    x, u0 = ffn(x, bsz, seq, tm, w['norm_ffn'][0], w['w_up'][0], w['ffn_conv_w'][0], w['ffn_conv_b'][0],
                w['w_down'][0], w['norm_final'], False)

    k, v, qkv = proj1(x, w['norm_mix'][1], w['w_in1'], tm)
    o = attn_prompt(qkv, bsz, seq, min(512, seq), w['lam4'], w['subln'], tk=min(256, seq))
    x = out_proj([o], [w['w_out1']], x, tm)
    y, u1 = ffn(x, bsz, seq, tm, w['norm_ffn'][1], w['w_up'][1], w['ffn_conv_w'][1], w['ffn_conv_b'][1],
                w['w_down'][1], w['norm_final'], True)

    p3 = p0.reshape(bsz, seq, D_IN0_PAD)
    rnn_conv = p3[:, seq - (CONV_W - 1):, 0:D_RNN]
    ssm_conv = p3[:, seq - (CONV_W - 1):, 2 * D_RNN + D_SSM:2 * D_RNN + D_SSM + SSM_CONV_DIM]
    ffn_conv = jnp.stack([u0[:, SUBLANES - (FFN_CONV_W - 1):], u1[:, SUBLANES - (FFN_CONV_W - 1):]])
    return (y.reshape(bsz, seq, d), rnn_conv, h_tail[:, SUBLANES - 1], ssm_conv, _state_untranspose(st),
            k.reshape(bsz, seq, ATT_HEADS, 2 * ATT_HEAD_DIM), v.reshape(bsz, seq, ATT_HEADS, ATT_V_DIM), ffn_conv)


def _to_groups(x_tok, hist=None):
    b, t, c = x_tok.shape
    if hist is None:
        lead = jnp.zeros((b, TOK0, c), x_tok.dtype)
    else:
        k = hist.shape[1]
        lead = jnp.concatenate([jnp.zeros((b, TOK0 - k, c), x_tok.dtype), hist], axis=1)
    return jnp.concatenate([lead, x_tok], axis=1).reshape(b * GROUP, c)


def _sample_trunk(x_sample, state_rnn_conv, state_rnn_h, state_ssm_conv, state_ssm, cache_k, cache_v,
                  state_ffn_conv, page_table, w):
    nb, t, d = x_sample.shape
    assert t == GROUP - TOK0
    m = nb * GROUP
    x = _to_groups(x_sample)

    p0 = proj0(x, w['norm_mix'][0], w['w_in0'], m)
    p0g = p0.reshape(nb, GROUP, D_IN0_PAD)
    c_xbc = 2 * D_RNN + D_SSM
    xr = _to_groups(p0g[:, TOK0:, 0:D_RNN], state_rnn_conv)
    p0r = jnp.concatenate([xr, p0[:, D_RNN:2 * D_RNN]], axis=1)
    hinj = _to_groups(jnp.zeros((nb, t, D_RNN), F32), state_rnn_h[:, None, :])
    rnn_out, h_all = rglru(p0r, 1, m, m, w['conv_rnn_w'], w['conv_rnn_b'], w['wa'], w['ba'], w['wx'], w['bx'],
                           w['lam'], hinj=hinj)
    lc = SSD_CHUNK
    padc = lambda a: jnp.pad(a, ((0, 0), (0, lc - t), (0, 0))).reshape(nb * lc, a.shape[-1])
    xbc_p = padc(p0g[:, TOK0:, c_xbc:c_xbc + SSM_CONV_DIM])
    z_p = padc(p0g[:, TOK0:, 2 * D_RNN:2 * D_RNN + D_SSM])
    dt_p = padc(p0g[:, TOK0:, D_IN0_PAD - LANES:])
    halo = jnp.concatenate([jnp.zeros((nb, HALO - (CONV_W - 1), SSM_CONV_DIM), F32), state_ssm_conv], axis=1)
    halo_spec = pl.BlockSpec((None, HALO, SSM_CONV_DIM), lambda b, i: (b, 0, 0))
    y_p, st = ssd(xbc_p, 0, halo, halo_spec, z_p, 0, dt_p, 0, nb, 1, lc, t, w['conv_ssm_w'], w['conv_ssm_b'],
                  w['dt_bias'], w['a_log'], w['d_skip'], w['ssm_norm'], s0=_state_t(state_ssm))
    y_ssm = _to_groups(y_p.reshape(nb, lc, D_SSM)[:, 0:t])
    x = out_proj([rnn_out, y_ssm], [w['w_out0a'], w['w_out0b']], x, m)
    uinj0 = _to_groups(jnp.zeros((nb, t, 2 * D_FF), F32), state_ffn_conv[0])
    x, u0 = ffn(x, 1, m, m, w['norm_ffn'][0], w['w_up'][0], w['ffn_conv_w'][0], w['ffn_conv_b'][0],
                w['w_down'][0], w['norm_final'], False, uinj=uinj0)

    k, v, qkv = proj1(x, w['norm_mix'][1], w['w_in1'], m)
    qg = qkv[:, 0:D_MODEL].reshape(nb, GROUP, ATT_HEADS, 2, ATT_HEAD_DIM)[:, TOK0:]
    flip = jnp.array([[0, 1], [1, 0]], dtype=BF16)
    qh = jnp.einsum('bqhmd,nm->bhnqmd', qg, flip).reshape(nb, ATT_HEADS * GROUP, 2 * ATT_HEAD_DIM)
    o = attn_sample(page_table, cache_k, cache_v, qh, k, v, w['lam4'], w['subln'], npp=8)
    x = out_proj([o], [w['w_out1']], x, m)
    uinj1 = _to_groups(jnp.zeros((nb, t, 2 * D_FF), F32), state_ffn_conv[1])
    y, u1 = ffn(x, 1, m, m, w['norm_ffn'][1], w['w_up'][1], w['ffn_conv_w'][1], w['ffn_conv_b'][1],
                w['w_down'][1], w['norm_final'], True, uinj=uinj1)

    tok = lambda a: a.reshape(nb, GROUP, a.shape[-1])[:, TOK0:]
    rnn_conv = p0g[:, GROUP - (CONV_W - 1):, 0:D_RNN]
    ssm_conv = p0g[:, GROUP - (CONV_W - 1):, c_xbc:c_xbc + SSM_CONV_DIM]
    ffn_conv = jnp.stack([u0.reshape(nb, GROUP, -1)[:, GROUP - (FFN_CONV_W - 1):],
                          u1.reshape(nb, GROUP, -1)[:, GROUP - (FFN_CONV_W - 1):]])
    return (tok(y), rnn_conv, h_all.reshape(nb, GROUP, D_RNN)[:, GROUP - 1], ssm_conv, _state_untranspose(st),
            tok(k).reshape(nb, t, ATT_HEADS, 2 * ATT_HEAD_DIM), tok(v).reshape(nb, t, ATT_HEADS, ATT_V_DIM),
            ffn_conv)


def kernel(x_prompt, x_sample, state_rnn_conv, state_rnn_h, state_ssm_conv, state_ssm, cache_k, cache_v, state_ffn_conv, page_table, norm_mix, norm_ffn, norm_final, w_in0, conv_rnn_w, conv_rnn_b, rg_w_a, rg_b_a, rg_w_x, rg_b_x, rg_lambda, conv_ssm_w, conv_ssm_b, dt_bias, a_log, d_skip, ssm_norm, w_out0, w_in1, lambda_q1, lambda_k1, lambda_q2, lambda_k2, subln, w_out1, w_up, ffn_conv_w, ffn_conv_b, w_down):
    w = _prep_weights(norm_mix, norm_ffn, norm_final, w_in0, conv_rnn_w, conv_rnn_b, rg_w_a, rg_b_a, rg_w_x,
                      rg_b_x, rg_lambda, conv_ssm_w, conv_ssm_b, dt_bias, a_log, d_skip, ssm_norm, w_out0, w_in1,
                      lambda_q1, lambda_k1, lambda_q2, lambda_k2, subln, w_out1, w_up, ffn_conv_w, ffn_conv_b,
                      w_down)
    p = _prompt_trunk(x_prompt, w)
    s = _sample_trunk(x_sample, state_rnn_conv, state_rnn_h, state_ssm_conv, state_ssm, cache_k, cache_v,
                      state_ffn_conv, page_table, w)
    return (p[0], s[0]) + p[1:] + s[1:]
```

```python
import functools
import math

import jax
import jax.numpy as jnp
import numpy as np
from jax import lax
from jax.experimental import pallas as pl
from jax.experimental.pallas import tpu as pltpu

F32 = jnp.float32
BF16 = jnp.bfloat16

D_MODEL = 1024
D_RNN = 1024
RNN_BW = 64
RG_C = 8.0
CONV_W = 4
SSM_HEADS = 16
SSM_HEAD_DIM = 64
D_SSM = SSM_HEADS * SSM_HEAD_DIM
SSM_GROUPS = 2
SSM_STATE = 128
SSM_CONV_DIM = D_SSM + 2 * SSM_GROUPS * SSM_STATE
SSD_CHUNK = 128
D_IN0 = 2 * D_RNN + D_SSM + SSM_CONV_DIM + SSM_HEADS
D_IN0_PAD = 2 * D_RNN + D_SSM + SSM_CONV_DIM + 128
ATT_HEADS = 8
ATT_HEAD_DIM = 64
ATT_V_DIM = 128
LAMBDA_INIT = 0.8 - 0.6 * math.exp(-0.3 * 1)
D_FF = 3 * D_MODEL
FFN_CONV_W = 3
EPS = 1e-6
LOG2E = math.log2(math.e)
PAGE_SIZE = 128

SUBLANES = 8
BF16_ROWS = 16
LANES = 128
HALO = SUBLANES
GROUP = 8
TOK0 = 4
VMEM_LIMIT = 56 * 1024 * 1024

ROW_TILE = 512
RNN_TILE = 256
COL_CHUNK = 512
ATT_Q_TILE = 512
ATT_K_TILE = 256
PAGES_PER_STEP = 16


def _cparams(sem):
    return pltpu.CompilerParams(dimension_semantics=sem, vmem_limit_bytes=VMEM_LIMIT)


def _resident(shape):
    nd = len(shape)
    return pl.BlockSpec(shape, lambda *_: (0,) * nd, pipeline_mode=pl.Buffered(1))


def _rms(x, g):
    var = jnp.mean(x * x, axis=-1, keepdims=True)
    return x * lax.rsqrt(var + EPS) * g


def _softplus(x):
    return jnp.maximum(x, 0.0) + jnp.log1p(jnp.exp(-jnp.abs(x)))


def _dot(a, b):
    return jnp.dot(a, b, preferred_element_type=F32)


def _dot_nt(a, b):
    return lax.dot_general(a, b, (((1,), (1,)), ((), ())), preferred_element_type=F32)


def _proj0_kernel(x_ref, g_ref, w_ref, o_ref, *, tn):
    xn = _rms(x_ref[...], g_ref[...]).astype(BF16)
    n = w_ref.shape[1]
    for c in range(0, n, tn):
        w = min(tn, n - c)
        o_ref[:, c:c + w] = _dot(xn, w_ref[:, c:c + w])


def proj0(x, g, w, tm):
    m, d = x.shape
    n = w.shape[1]
    return pl.pallas_call(
        functools.partial(_proj0_kernel, tn=COL_CHUNK),
        grid=(m // tm,),
        in_specs=[pl.BlockSpec((tm, d), lambda i: (i, 0)), _resident((1, d)), _resident((d, n))],
        out_specs=pl.BlockSpec((tm, n), lambda i: (i, 0)),
        out_shape=jax.ShapeDtypeStruct((m, n), F32),
        compiler_params=_cparams(("parallel",)),
        name="proj0",
    )(x, g, w)


def _proj1_kernel(x_ref, g_ref, w_ref, k_ref, v_ref, qkv_ref, *, tn):
    xn = _rms(x_ref[...], g_ref[...]).astype(BF16)
    d = k_ref.shape[1]
    for c in range(0, 3 * d, tn):
        r = _dot(xn, w_ref[:, c:c + tn])
        if c < d:
            qkv_ref[:, c:c + tn] = (r * (ATT_HEAD_DIM ** -0.5 * LOG2E)).astype(BF16)
        else:
            qkv_ref[:, c:c + tn] = r.astype(BF16)
            if c < 2 * d:
                k_ref[:, c - d:c - d + tn] = r
            else:
                v_ref[:, c - 2 * d:c - 2 * d + tn] = r


def proj1(x, g, w, tm):
    m, d = x.shape
    n = w.shape[1]
    return pl.pallas_call(
        functools.partial(_proj1_kernel, tn=COL_CHUNK),
        grid=(m // tm,),
        in_specs=[pl.BlockSpec((tm, d), lambda i: (i, 0)), _resident((1, d)), _resident((d, n))],
        out_specs=[pl.BlockSpec((tm, d), lambda i: (i, 0)), pl.BlockSpec((tm, d), lambda i: (i, 0)),
                   pl.BlockSpec((tm, n), lambda i: (i, 0))],
        out_shape=[jax.ShapeDtypeStruct((m, d), F32), jax.ShapeDtypeStruct((m, d), F32),
                   jax.ShapeDtypeStruct((m, n), BF16)],
        compiler_params=_cparams(("parallel",)),
        name="proj1",
    )(x, g, w)


def _out_proj_kernel(*refs, n_in, tn):
    a_refs = refs[:n_in]
    w_refs = refs[n_in:2 * n_in]
    res_ref = refs[2 * n_in]
    o_ref = refs[2 * n_in + 1]
    n = o_ref.shape[1]
    for c in range(0, n, tn):
        acc = res_ref[:, c:c + tn]
        for a_ref, w_ref in zip(a_refs, w_refs):
            acc = acc + _dot(a_ref[...], w_ref[:, c:c + tn])
        o_ref[:, c:c + tn] = acc


def out_proj(a_list, w_list, res, tm):
    m, n = res.shape
    n_in = len(a_list)
    in_specs = [pl.BlockSpec((tm, a.shape[1]), lambda i: (i, 0)) for a in a_list]
    in_specs += [_resident(w.shape) for w in w_list]
    in_specs += [pl.BlockSpec((tm, n), lambda i: (i, 0))]
    return pl.pallas_call(
        functools.partial(_out_proj_kernel, n_in=n_in, tn=COL_CHUNK),
        grid=(m // tm,),
        in_specs=in_specs,
        out_specs=pl.BlockSpec((tm, n), lambda i: (i, 0)),
        out_shape=jax.ShapeDtypeStruct((m, n), F32),
        compiler_params=_cparams(("parallel",)),
        name="out_proj",
    )(*a_list, *w_list, res)


def _conv_from_ext(ext_ref, w_ref, b_ref, width, tm, c0, c1):
    y = b_ref[:, c0:c1]
    for j in range(width):
        off = HALO - (width - 1) + j
        y = y + ext_ref[off:off + tm, c0:c1] * w_ref[j:j + 1, c0:c1]
    return y


def _scan_rows(a, b, c):
    tm = a.shape[0]
    rowm = lax.broadcasted_iota(jnp.int32, a.shape, 0) % SUBLANES
    s = 1
    while s < SUBLANES:
        keep = rowm >= s
        a_s = jnp.where(keep, pltpu.roll(a, s, axis=0), 1.0)
        b_s = jnp.where(keep, pltpu.roll(b, s, axis=0), 0.0)
        b = a * b_s + b
        a = a * a_s
        s *= 2
    hs = []
    for g in range(tm // SUBLANES):
        hg = a[g * SUBLANES:(g + 1) * SUBLANES] * c + b[g * SUBLANES:(g + 1) * SUBLANES]
        c = hg[SUBLANES - 1:SUBLANES, :]
        hs.append(hg)
    return jnp.concatenate(hs, axis=0)


def _rglru_kernel(*refs, tm, sample):
    if sample:
        (x_ref, y_ref, cw_ref, cb_ref, wa_ref, ba_ref, wx_ref, bx_ref, lam_ref, hinj_ref,
         out_ref, h_out_ref, ext_ref, carry_ref) = refs
    else:
        (x_ref, y_ref, halo_ref, cw_ref, cb_ref, wa_ref, ba_ref, wx_ref, bx_ref, lam_ref,
         out_ref, h_out_ref, ext_ref, carry_ref) = refs
    i = pl.program_id(1)
    d = x_ref.shape[1]

    @pl.when(i == 0)
    def _():
        carry_ref[...] = jnp.zeros_like(carry_ref)

    if sample:
        ext_ref[0:HALO, :] = jnp.zeros((HALO, d), F32)
    else:
        ext_ref[0:HALO, :] = jnp.where(i == 0, 0.0, halo_ref[...])
    ext_ref[HALO:HALO + tm, :] = x_ref[...]

    row = lax.broadcasted_iota(jnp.int32, (tm, 1), 0)
    tail = h_out_ref.shape[0]
    sp = _softplus(-lam_ref[...])
    cbw = 4 * RNN_BW
    for c in range(d // cbw):
        c0, c1 = c * cbw, (c + 1) * cbw
        xr = _conv_from_ext(ext_ref, cw_ref, cb_ref, CONV_W, tm, c0, c1)
        xb = xr.astype(BF16)
        r = jax.nn.sigmoid(_dot(xb, wa_ref[c]) + ba_ref[:, c0:c1])
        ig = jax.nn.sigmoid(_dot(xb, wx_ref[c]) + bx_ref[:, c0:c1])
        log_a = -RG_C * r * sp[:, c0:c1]
        a = jnp.exp(log_a)
        mult = jnp.sqrt(jnp.maximum(1.0 - a * a, 0.0))
        if sample:
            inj = (row % GROUP) == (TOK0 - 1)
            a = jnp.where(inj, 0.0, a)
            bv = jnp.where(inj, hinj_ref[:, c0:c1], xr * ig * mult)
        else:
            first = jnp.logical_and(i == 0, row == 0)
            a = jnp.where(first, 0.0, a)
            mult = jnp.where(first, 1.0, mult)
            bv = xr * ig * mult
        for k in range(cbw // LANES):
            l0, l1 = c0 + k * LANES, c0 + (k + 1) * LANES
            h = _scan_rows(a[:, k * LANES:(k + 1) * LANES], bv[:, k * LANES:(k + 1) * LANES], carry_ref[:, l0:l1])
            carry_ref[:, l0:l1] = h[tm - 1:tm, :]
            out_ref[:, l0:l1] = (h * jax.nn.gelu(y_ref[:, l0:l1])).astype(out_ref.dtype)
            h_out_ref[:, l0:l1] = h[tm - tail:tm, :]


def rglru(proj, nseq, seq, tm, cw, cb, wa, ba, wx, bx, lam, hinj=None):
    sample = hinj is not None
    nt = seq // tm
    d = D_RNN
    xmap = lambda b, i: (b * nt + i, 0)
    in_specs = [pl.BlockSpec((tm, d), xmap), pl.BlockSpec((tm, d), lambda b, i: (b * nt + i, 1))]
    args = [proj, proj]
    if not sample:
        hb = tm // HALO
        in_specs.append(pl.BlockSpec((HALO, d), lambda b, i: (jnp.maximum((b * nt + i) * hb - 1, 0), 0)))
        args.append(proj)
    in_specs += [_resident(cw.shape), _resident(cb.shape), _resident(wa.shape), _resident(ba.shape),
                 _resident(wx.shape), _resident(bx.shape), _resident(lam.shape)]
    args += [cw, cb, wa, ba, wx, bx, lam]
    if sample:
        in_specs.append(pl.BlockSpec((tm, d), xmap))
        args.append(hinj)
        h_spec = pl.BlockSpec((tm, d), xmap)
        h_shape = jax.ShapeDtypeStruct((nseq * seq, d), F32)
    else:
        h_spec = pl.BlockSpec((None, SUBLANES, d), lambda b, i: (b, 0, 0))
        h_shape = jax.ShapeDtypeStruct((nseq, SUBLANES, d), F32)
    return pl.pallas_call(
        functools.partial(_rglru_kernel, tm=tm, sample=sample),
        grid=(nseq, nt),
        in_specs=in_specs,
        out_specs=[pl.BlockSpec((tm, d), xmap), h_spec],
        out_shape=[jax.ShapeDtypeStruct((nseq * seq, d), BF16), h_shape],
        scratch_shapes=[pltpu.VMEM((tm + HALO, d), F32), pltpu.VMEM((1, d), F32)],
        compiler_params=_cparams(("parallel", "arbitrary")),
        name="rglru",
    )(*args)


def _pair(v, hp, lane_lo):
    return jnp.where(lane_lo, v[:, 2 * hp:2 * hp + 1], v[:, 2 * hp + 1:2 * hp + 2])


def _ssd_kernel(*refs, lc, n_valid, has_init):
    if has_init:
        (xbc_ref, halo_ref, z_ref, dt_ref, cw_ref, cb_ref, dtb_ref, alog_ref, dsk_ref, nrm_ref, s0_ref,
         y_ref, sout_ref, ext_ref, st_ref, yacc_ref) = refs
    else:
        (xbc_ref, halo_ref, z_ref, dt_ref, cw_ref, cb_ref, dtb_ref, alog_ref, dsk_ref, nrm_ref,
         y_ref, sout_ref, ext_ref, st_ref, yacc_ref) = refs
    i = pl.program_id(1)
    n = SSM_STATE
    hd2 = 2 * SSM_HEAD_DIM
    gw = D_SSM // SSM_GROUPS

    @pl.when(i == 0)
    def _():
        if has_init:
            st_ref[...] = s0_ref[...]
        else:
            st_ref[...] = jnp.zeros_like(st_ref)

    if has_init:
        ext_ref[0:HALO, :] = halo_ref[...]
    else:
        ext_ref[0:HALO, :] = jnp.where(i == 0, 0.0, halo_ref[...])
    ext_ref[HALO:HALO + lc, :] = xbc_ref[...]
    xbc = _conv_from_ext(ext_ref, cw_ref, cb_ref, CONV_W, lc, 0, SSM_CONV_DIM)
    xbc = xbc * jax.nn.sigmoid(xbc)
    xs = xbc[:, 0:D_SSM]

    row = lax.broadcasted_iota(jnp.int32, (lc, 1), 0)
    dt = _softplus(dt_ref[...] + dtb_ref[...])
    if n_valid < lc:
        dt = jnp.where(row < n_valid, dt, 0.0)
    acs = dt * (-jnp.exp(alog_ref[...]))
    s = 1
    while s < lc:
        acs = acs + jnp.where(row >= s, pltpu.roll(acs, s, axis=0), 0.0)
        s *= 2
    if lc == LANES:
        acs_t = acs.T
    else:
        acs_t = jnp.concatenate([acs, jnp.zeros((LANES - lc, LANES), F32)], axis=0).T[:, 0:lc]
    a_last = acs[lc - 1:lc, :]
    ea = jnp.exp(acs)
    dsv = jnp.exp(a_last - acs)
    cdec = jnp.exp(a_last)

    lane_lo = lax.broadcasted_iota(jnp.int32, (1, hd2), 1) < SSM_HEAD_DIM
    causal = (lax.broadcasted_iota(jnp.int32, (lc, lc), 0) >= lax.broadcasted_iota(jnp.int32, (lc, lc), 1))
    for g in range(SSM_GROUPS):
        bm = xbc[:, D_SSM + g * n:D_SSM + (g + 1) * n]
        cm = xbc[:, D_SSM + SSM_GROUPS * n + g * n:D_SSM + SSM_GROUPS * n + (g + 1) * n]
        bm_b = bm.astype(BF16)
        cm_b = cm.astype(BF16)
        cbm = _dot_nt(cm_b, bm_b)
        st_old = st_ref[:, g * gw:(g + 1) * gw]
        yoff = _dot(cm_b, st_old.astype(BF16))
        if lc == LANES:
            bm_t = bm.T.astype(BF16)
        else:
            bm_t = jnp.concatenate([bm, jnp.zeros((LANES - lc, n), F32)], axis=0).T[:, 0:lc].astype(BF16)
        w_parts = []
        cd_parts = []
        for pp in range(gw // hd2):
            hp = g * (gw // hd2) + pp
            h0, h1 = 2 * hp, 2 * hp + 1
            c0 = hp * hd2
            xg = xs[:, c0:c0 + hd2] * _pair(dt, hp, lane_lo)
            xg_b = xg.astype(BF16)
            yd = []
            for h in (h0, h1):
                seg = acs[:, h:h + 1] - acs_t[h:h + 1, :]
                dec = jnp.exp(jnp.where(causal, seg, -jnp.inf))
                yd.append(_dot((cbm * dec).astype(BF16), xg_b))
            ydiag = jnp.where(lane_lo, yd[0], yd[1])
            yo = yoff[:, pp * hd2:(pp + 1) * hd2] * _pair(ea, hp, lane_lo)
            yacc_ref[:, c0:c0 + hd2] = ydiag + yo + dsk_ref[:, c0:c0 + hd2] * xs[:, c0:c0 + hd2]
            w_parts.append((xg * _pair(dsv, hp, lane_lo)).astype(BF16))
            cd_parts.append(_pair(cdec, hp, lane_lo))
        w_all = jnp.concatenate(w_parts, axis=1)
        cd_all = jnp.concatenate(cd_parts, axis=1)
        st_ref[:, g * gw:(g + 1) * gw] = cd_all * st_old + _dot(bm_t, w_all)

    z = z_ref[...]
    y = yacc_ref[...] * (z * jax.nn.sigmoid(z))
    y_ref[...] = _rms(y, nrm_ref[...]).astype(y_ref.dtype)
    sout_ref[...] = st_ref[...]


def ssd(xbc_arr, xbc_col, halo_arr, halo_map, z_arr, z_col, dt_arr, dt_col, nseq, nchunk, lc, n_valid,
        cw, cb, dtb, alog, dsk, nrm, s0=None):
    has_init = s0 is not None
    rmap = lambda col: (lambda b, i: (b * nchunk + i, col))
    in_specs = [pl.BlockSpec((lc, SSM_CONV_DIM), rmap(xbc_col)),
                halo_map,
                pl.BlockSpec((lc, D_SSM), rmap(z_col)),
                pl.BlockSpec((lc, LANES), rmap(dt_col)),
                _resident(cw.shape), _resident(cb.shape), _resident(dtb.shape), _resident(alog.shape),
                _resident(dsk.shape), _resident(nrm.shape)]
    args = [xbc_arr, halo_arr, z_arr, dt_arr, cw, cb, dtb, alog, dsk, nrm]
    if has_init:
        in_specs.append(pl.BlockSpec((None, SSM_STATE, D_SSM), lambda b, i: (b, 0, 0)))
        args.append(s0)
    return pl.pallas_call(
        functools.partial(_ssd_kernel, lc=lc, n_valid=n_valid, has_init=has_init),
        grid=(nseq, nchunk),
        in_specs=in_specs,
        out_specs=[pl.BlockSpec((lc, D_SSM), lambda b, i: (b * nchunk + i, 0)),
                   pl.BlockSpec((None, SSM_STATE, D_SSM), lambda b, i: (b, 0, 0))],
        out_shape=[jax.ShapeDtypeStruct((nseq * nchunk * lc, D_SSM), BF16),
                   jax.ShapeDtypeStruct((nseq, SSM_STATE, D_SSM), F32)],
        scratch_shapes=[pltpu.VMEM((lc + HALO, SSM_CONV_DIM), F32), pltpu.VMEM((SSM_STATE, D_SSM), F32),
                        pltpu.VMEM((lc, D_SSM), F32)],
        compiler_params=_cparams(("parallel", "arbitrary")),
        name="ssd",
    )(*args)


def _ffn_kernel(*refs, tm, fc, sample, final_norm):
    if sample:
        (x_ref, g_ref, wup_ref, cw_ref, cb_ref, wdn_ref, gf_ref, uinj_ref,
         o_ref, u_out_ref, ext_ref) = refs
    else:
        (x_ref, halo_ref, g_ref, wup_ref, cw_ref, cb_ref, wdn_ref, gf_ref,
         o_ref, u_out_ref, ext_ref) = refs
    i = pl.program_id(1)
    d_ff = wdn_ref.shape[0]
    x = x_ref[...]
    hn = _rms(x, g_ref[...]).astype(BF16)
    if sample:
        row = lax.broadcasted_iota(jnp.int32, (tm, 1), 0) % GROUP
        hist = jnp.logical_and(row >= TOK0 - (FFN_CONV_W - 1), row < TOK0)
    else:
        halo_n = _rms(halo_ref[...], g_ref[...]).astype(BF16)
    tail = u_out_ref.shape[0]

    def up(c):
        out = []
        for base in (c, d_ff + c):
            u = _dot(hn, wup_ref[:, base:base + fc])
            uh = None if sample else _dot(halo_n, wup_ref[:, base:base + fc])
            out.append((u, uh))
        return out

    acc = x
    ups = up(0)
    for c in range(0, d_ff, fc):
        ups_next = up(c + fc) if c + fc < d_ff else None
        conv = []
        for n, base in enumerate((c, d_ff + c)):
            u, uh = ups[n]
            e_ref = ext_ref.at[n]
            if sample:
                u = jnp.where(hist, uinj_ref[:, base:base + fc], u)
                e_ref[0:HALO, :] = jnp.zeros((HALO, fc), F32)
            else:
                e_ref[0:HALO, :] = jnp.where(i == 0, 0.0, uh)
            e_ref[HALO:HALO + tm, :] = u
            u_out_ref[:, base:base + fc] = e_ref[HALO + tm - tail:HALO + tm, :]
            y = cb_ref[:, base:base + fc]
            for j in range(FFN_CONV_W):
                off = HALO - (FFN_CONV_W - 1) + j
                y = y + e_ref[off:off + tm, :] * cw_ref[j:j + 1, base:base + fc]
            conv.append(y)
        act = (jax.nn.gelu(conv[0]) * conv[1]).astype(BF16)
        acc = acc + _dot(act, wdn_ref[c:c + fc, :])
        ups = ups_next
    if final_norm:
        acc = _rms(acc, gf_ref[...])
    o_ref[...] = acc


def ffn(x, nseq, seq, tm, g, wup, cw, cb, wdn, gf, final_norm, uinj=None):
    sample = uinj is not None
    nt = seq // tm
    d = x.shape[1]
    d_up = wup.shape[1]
    xmap = lambda b, i: (b * nt + i, 0)
    in_specs = [pl.BlockSpec((tm, d), xmap)]
    args = [x]
    if not sample:
        hb = tm // HALO
        in_specs.append(pl.BlockSpec((HALO, d), lambda b, i: (jnp.maximum((b * nt + i) * hb - 1, 0), 0)))
        args.append(x)
    in_specs += [_resident(g.shape), _resident(wup.shape), _resident(cw.shape), _resident(cb.shape),
                 _resident(wdn.shape), _resident(gf.shape)]
    args += [g, wup, cw, cb, wdn, gf]
    if sample:
        in_specs.append(pl.BlockSpec((tm, d_up), xmap))
        args.append(uinj)
        u_spec = pl.BlockSpec((tm, d_up), xmap)
        u_shape = jax.ShapeDtypeStruct((nseq * seq, d_up), F32)
    else:
        u_spec = pl.BlockSpec((None, SUBLANES, d_up), lambda b, i: (b, 0, 0))
        u_shape = jax.ShapeDtypeStruct((nseq, SUBLANES, d_up), F32)
    fc = COL_CHUNK
    return pl.pallas_call(
        functools.partial(_ffn_kernel, tm=tm, fc=fc, sample=sample, final_norm=final_norm),
        grid=(nseq, nt),
        in_specs=in_specs,
        out_specs=[pl.BlockSpec((tm, d), xmap), u_spec],
        out_shape=[jax.ShapeDtypeStruct((nseq * seq, d), F32), u_shape],
        scratch_shapes=[pltpu.VMEM((2, tm + HALO, fc), F32)],
        compiler_params=_cparams(("parallel", "arbitrary")),
        name="ffn",
    )(*args)


def _lambda_full(lam_ref):
    lq1 = lam_ref[0:1, :]
    lk1 = lam_ref[1:2, :]
    lq2 = lam_ref[2:3, :]
    lk2 = lam_ref[3:4, :]
    return (jnp.exp(jnp.sum(lq1 * lk1, axis=-1, keepdims=True))
            - jnp.exp(jnp.sum(lq2 * lk2, axis=-1, keepdims=True)) + LAMBDA_INIT)


def _alibi_slopes():
    return np.asarray(2.0 ** (-8.0 * np.arange(1, ATT_HEADS + 1) / ATT_HEADS), dtype=np.float32)


def _attn_prompt_kernel(q_ref, k_ref, v_ref, slope_ref, lam_ref, subt_ref, o_ref, vt_ref, bias_ref, s_ref, p_ref,
                        m_ref, acc_ref, *, t, tk, unroll):
    h = pl.program_id(1)
    i = pl.program_id(2)
    dh = ATT_HEAD_DIM
    hw = 2 * dh
    nkb = vt_ref.shape[0]
    ns = t // LANES
    nd = t // tk

    @pl.when(i == 0)
    def _():
        for jb in range(nkb):
            for c in range(tk // LANES):
                r0 = jb * tk + c * LANES
                vt_ref[jb, 0:hw, c * LANES:(c + 1) * LANES] = v_ref[r0:r0 + LANES, :].astype(F32).T.astype(BF16)
            ones_row = lax.broadcasted_iota(jnp.int32, (BF16_ROWS, tk), 0) == 0
            vt_ref[jb, hw:hw + BF16_ROWS, :] = jnp.where(ones_row, 1.0, 0.0).astype(BF16)

    slope = slope_ref[pl.ds(h, 1), :][:, 0:1]

    @pl.when(i == 0)
    def _():
        kk = lax.broadcasted_iota(jnp.int32, (tk, t), 0)
        qi = lax.broadcasted_iota(jnp.int32, (tk, t), 1)
        b = (kk - qi).astype(F32) * (slope * LOG2E)
        bias_ref[...] = jnp.concatenate([b, b], axis=1)

    q = q_ref[...]
    lane = lax.broadcasted_iota(jnp.int32, (1, 2 * dh), 1)
    zero = jnp.zeros_like(q)
    qq = jnp.concatenate([jnp.where(lane < dh, q, zero), jnp.where(lane >= dh, q, zero)], axis=0)

    m_ref[...] = jnp.full_like(m_ref, -jnp.inf)
    acc_ref[...] = jnp.zeros_like(acc_ref)

    def put_scores(j, slot):
        s_ref[slot] = _dot_nt(k_ref[pl.ds(pl.multiple_of(j * tk, tk), tk), :], qq) + bias_ref[...]

    def strip(slot, pslot, c, off, d):
        cb = c % ns
        sl = slice(c * LANES, (c + 1) * LANES)
        nk, masked = tk, False
        if d is not None:
            nk = min(tk, max(0, (cb + 1) * LANES - d * tk))
            masked = d * tk + nk > cb * LANES
        if nk == 0:
            p_ref[pslot, :, sl] = jnp.zeros((tk, LANES), BF16)
            return jnp.ones((1, LANES), F32)
        if masked:
            kidx = lax.broadcasted_iota(jnp.int32, (nk, LANES), 0) + d * tk
            qidx = lax.broadcasted_iota(jnp.int32, (nk, LANES), 1) + cb * LANES
            causal = kidx <= qidx
        s1 = s_ref[slot, 0:nk, sl]
        if masked:
            s1 = jnp.where(causal, s1, -jnp.inf)
        m_old = m_ref[:, sl]
        m_new = jnp.maximum(m_old, jnp.max(s1, axis=0, keepdims=True) + off)
        m_ref[:, sl] = m_new
        s2 = s_ref[slot, 0:nk, sl] - (m_new - off)
        if masked:
            s2 = jnp.where(causal, s2, -jnp.inf)
        p = jnp.exp2(s2)
        alpha = jnp.exp2(m_old - m_new)
        p_ref[pslot, 0:nk, sl] = p.astype(BF16)
        if nk < tk:
            p_ref[pslot, nk:tk, sl] = jnp.zeros((tk - nk, LANES), BF16)
        return alpha

    def run(js, kinds, ahead):
        order = list(js) + list(ahead)
        pending = None
        for n, (j, d) in enumerate(zip(js, kinds)):
            if n + 2 < len(order):
                put_scores(order[n + 2], (n + 2) % 4)
            off = (slope * LOG2E) * jnp.full((1, 1), j * tk - i * t, jnp.int32).astype(F32)
            alpha = jnp.concatenate([strip(n % 4, n % 4, c, off, d) for c in range(2 * ns)], axis=1)
            pv = _dot(vt_ref[j], p_ref[n % 4])
            if pending is not None:
                acc_ref[...] = pending[0] * acc_ref[...] + pending[1]
            pending = (alpha, pv)
        acc_ref[...] = pending[0] * acc_ref[...] + pending[1]

    nfull = i * nd
    ng = nfull // unroll

    def body(g, carry):
        base = g * unroll
        run([base + u for u in range(unroll)], [None] * unroll, [base + unroll, base + unroll + 1])
        return carry

    put_scores(0, 0)
    put_scores(1, 1)
    lax.fori_loop(0, ng, body, 0)
    for rem in range(0, unroll, nd):

        @pl.when(nfull - ng * unroll == rem)
        def _(rem=rem):
            run([ng * unroll + u for u in range(rem)] + [nfull + d for d in range(nd)],
                [None] * rem + list(range(nd)), [])

    lam = _lambda_full(lam_ref)
    o = acc_ref[0:hw, :] / acc_ref[hw:hw + 1, :]
    o = o[:, 0:t] - lam * o[:, t:2 * t]
    var = jnp.mean(o * o, axis=0, keepdims=True)
    o = o * lax.rsqrt(var + EPS) * subt_ref[:, 0:1] * (1.0 - LAMBDA_INIT)
    for c in range(ns):
        o_ref[c * LANES:(c + 1) * LANES, :] = o[:, c * LANES:(c + 1) * LANES].T.astype(o_ref.dtype)


def attn_prompt(qkv, nseq, seq, t, lam4, subln, tk=None):
    tk = t if tk is None else tk
    assert t % tk == 0 and 4 % (t // tk) == 0 and seq >= 2 * tk
    nq = seq // t
    hw = 2 * ATT_HEAD_DIM
    qkv3 = qkv.reshape(nseq, seq, qkv.shape[1])
    slopes = jnp.asarray(np.tile(_alibi_slopes()[:, None], (1, LANES)))
    subt = jnp.tile(subln.reshape(hw, 1), (1, LANES))
    return pl.pallas_call(
        functools.partial(_attn_prompt_kernel, t=t, tk=tk, unroll=4),
        grid=(nseq, ATT_HEADS, nq),
        in_specs=[pl.BlockSpec((None, t, hw), lambda b, h, i: (b, i, h)),
                  pl.BlockSpec((None, seq, hw), lambda b, h, i: (b, 0, ATT_HEADS + h)),
                  pl.BlockSpec((None, seq, hw), lambda b, h, i: (b, 0, 2 * ATT_HEADS + h)),
                  _resident(slopes.shape), _resident(lam4.shape), _resident(subt.shape)],
        out_specs=pl.BlockSpec((None, t, hw), lambda b, h, i: (b, i, h)),
        out_shape=jax.ShapeDtypeStruct((nseq, seq, ATT_HEADS * ATT_V_DIM), BF16),
        scratch_shapes=[pltpu.VMEM((seq // tk, hw + BF16_ROWS, tk), BF16), pltpu.VMEM((tk, 2 * t), F32),
                        pltpu.VMEM((4, tk, 2 * t), F32), pltpu.VMEM((4, tk, 2 * t), BF16),
                        pltpu.VMEM((1, 2 * t), F32), pltpu.VMEM((hw + BF16_ROWS, 2 * t), F32)],
        compiler_params=_cparams(("parallel", "parallel", "arbitrary")),
        name="attn_prompt",
    )(qkv3, qkv3, qkv3, slopes, lam4, subt).reshape(nseq * seq, ATT_HEADS * ATT_V_DIM)


def _attn_sample_kernel(pt_ref, *refs, npp, past):
    k_refs = refs[:npp]
    v_refs = refs[npp:2 * npp]
    (q_ref, knew_ref, vnew_ref, slope_ref, lam_ref, sub_ref, o_ref, b0_ref, m_ref, l_ref, acc_ref) = refs[2 * npp:]
    j = pl.program_id(1)
    nj = pl.num_programs(1)
    nr = q_ref.shape[0]
    ncol = PAGE_SIZE * ATT_HEADS
    rowi = lax.broadcasted_iota(jnp.int32, (nr, 1), 0)
    slope = slope_ref[:, 0:1] * LOG2E
    tq = (rowi % TOK0).astype(F32)

    @pl.when(j == 0)
    def _():
        m_ref[...] = jnp.full_like(m_ref, -jnp.inf)
        l_ref[...] = jnp.zeros_like(l_ref)
        acc_ref[...] = jnp.zeros_like(acc_ref)
        col = lax.broadcasted_iota(jnp.int32, (1, ncol), 1)
        same_head = (col % ATT_HEADS) == (rowi // GROUP)
        b0_ref[...] = jnp.where(same_head, slope * ((col // ATT_HEADS).astype(F32) - tq), -jnp.inf)

    q = q_ref[...]
    ss = [_dot_nt(q, k_refs[r][...].astype(BF16)) for r in range(npp)]
    stats = []
    for s in ss:
        s = s + b0_ref[...]
        mr = jnp.max(s, axis=-1, keepdims=True)
        p = jnp.exp2(s - mr)
        stats.append((mr, jnp.sum(p, axis=-1, keepdims=True), p.astype(BF16)))
    parts = []
    for r, (mr, lr, p) in enumerate(stats):
        pv = _dot(p, v_refs[r][...].astype(BF16))
        page0 = jnp.full((1, 1), (j * npp + r) * PAGE_SIZE, jnp.int32).astype(F32)
        parts.append((mr + slope * (page0 - past), lr, pv))
    m_old = m_ref[...]
    m_new = m_old
    for mt, _, _ in parts:
        m_new = jnp.maximum(m_new, mt)
    alpha = jnp.exp2(m_old - m_new)
    l_acc = alpha * l_ref[...]
    acc = alpha * acc_ref[...]
    for mt, lr, pv in parts:
        wgt = jnp.exp2(mt - m_new)
        l_acc = l_acc + wgt * lr
        acc = acc + wgt * pv
    m_ref[...] = m_new
    l_ref[...] = l_acc
    acc_ref[...] = acc

    @pl.when(j == nj - 1)
    def _():
        nn = knew_ref.shape[0]
        zpad = jnp.zeros((LANES - nn, LANES), F32)
        kb = jnp.concatenate([knew_ref[...], zpad], axis=0).astype(BF16)
        vb = jnp.concatenate([vnew_ref[...], zpad], axis=0).astype(BF16)
        col = lax.broadcasted_iota(jnp.int32, (1, LANES), 1)
        grow = col // ATT_HEADS
        tk = (grow - TOK0).astype(F32)
        ok = jnp.logical_and(jnp.logical_and(grow >= TOK0, grow < GROUP), tk <= tq)
        ok = jnp.logical_and(ok, (col % ATT_HEADS) == (rowi // GROUP))
        s = jnp.where(ok, _dot_nt(q, kb) - slope * (tq - tk), -jnp.inf)
        m_o = m_ref[...]
        m_n = jnp.maximum(m_o, jnp.max(s, axis=-1, keepdims=True))
        p = jnp.exp2(s - m_n)
        al = jnp.exp2(m_o - m_n)
        l_fin = al * l_ref[...] + jnp.sum(p, axis=-1, keepdims=True)
        acc_fin = (al * acc_ref[...] + _dot(p.astype(BF16), vb)) / l_fin

        lam = _lambda_full(lam_ref)
        rows8 = lax.broadcasted_iota(jnp.int32, (GROUP, 1), 0)
        for h in range(ATT_HEADS):
            blk = acc_fin[h * GROUP:(h + 1) * GROUP, :]
            o = blk - lam * pltpu.roll(blk, TOK0, axis=0)
            o = _rms(o, sub_ref[...]) * (1.0 - LAMBDA_INIT)
            o_ref[:, h * ATT_V_DIM:(h + 1) * ATT_V_DIM] = jnp.where(rows8 >= TOK0, o, 0.0).astype(o_ref.dtype)


def attn_sample(page_table, cache_k, cache_v, qh, k_new, v_new, lam4, subln, npp):
    nb, n_pages = page_table.shape
    n_pool = cache_k.shape[0]
    dk = ATT_HEADS * ATT_V_DIM
    ncol = PAGE_SIZE * ATT_HEADS
    ck = cache_k.reshape(n_pool, ncol, 2 * ATT_HEAD_DIM)
    cv = cache_v.reshape(n_pool, ncol, ATT_V_DIM)
    nr = qh.shape[1]
    nn = GROUP * ATT_HEADS
    slopes = jnp.asarray(np.tile(np.repeat(_alibi_slopes(), GROUP)[:, None], (1, LANES)))

    def page_spec(r):
        return pl.BlockSpec((None, ncol, LANES), lambda b, j, pt: (pt[b, j * npp + r], 0, 0))

    in_specs = [page_spec(r) for r in range(npp)] + [page_spec(r) for r in range(npp)]
    in_specs += [pl.BlockSpec((None, nr, LANES), lambda b, j, pt: (b, 0, 0)),
                 pl.BlockSpec((nn, LANES), lambda b, j, pt: (b, 0)),
                 pl.BlockSpec((nn, LANES), lambda b, j, pt: (b, 0)),
                 pl.BlockSpec(slopes.shape, lambda b, j, pt: (0, 0)),
                 pl.BlockSpec(lam4.shape, lambda b, j, pt: (0, 0)),
                 pl.BlockSpec(subln.shape, lambda b, j, pt: (0, 0))]
    grid_spec = pltpu.PrefetchScalarGridSpec(
        num_scalar_prefetch=1,
        grid=(nb, n_pages // npp),
        in_specs=in_specs,
        out_specs=pl.BlockSpec((GROUP, dk), lambda b, j, pt: (b, 0)),
        scratch_shapes=[pltpu.VMEM((nr, ncol), F32), pltpu.VMEM((nr, 1), F32), pltpu.VMEM((nr, 1), F32),
                        pltpu.VMEM((nr, LANES), F32)],
    )
    return pl.pallas_call(
        functools.partial(_attn_sample_kernel, npp=npp, past=float(n_pages * PAGE_SIZE)),
        grid_spec=grid_spec,
        out_shape=jax.ShapeDtypeStruct((nb * GROUP, dk), BF16),
        compiler_params=_cparams(("parallel", "arbitrary")),
        name="attn_sample",
    )(page_table, *([ck] * npp), *([cv] * npp), qh, k_new.reshape(nb * nn, LANES), v_new.reshape(nb * nn, LANES),
      slopes, lam4, subln)


def _block_diag4(w):
    nb = w.shape[0] // 4
    eye = jnp.eye(4, dtype=w.dtype)
    w4 = w.reshape(nb, 4, RNN_BW, RNN_BW)
    return jnp.einsum('cajk,ab->cajbk', w4, eye).reshape(nb, 4 * RNN_BW, 4 * RNN_BW).astype(BF16)


def _prep_weights(norm_mix, norm_ffn, norm_final, w_in0, conv_rnn_w, conv_rnn_b, rg_w_a, rg_b_a, rg_w_x, rg_b_x,
                  rg_lambda, conv_ssm_w, conv_ssm_b, dt_bias, a_log, d_skip, ssm_norm, w_out0, w_in1, lambda_q1,
                  lambda_k1, lambda_q2, lambda_k2, subln, w_out1, w_up, ffn_conv_w, ffn_conv_b, w_down):
    row = lambda v: v.reshape(1, -1)
    pad_l = lambda v: jnp.pad(v.reshape(1, -1), ((0, 0), (0, LANES - v.shape[-1])))
    return dict(
        norm_mix=[row(norm_mix[l]) for l in range(2)],
        norm_ffn=[row(norm_ffn[l]) for l in range(2)],
        norm_final=row(norm_final),
        w_in0=jnp.pad(w_in0, ((0, 0), (0, D_IN0_PAD - D_IN0))).astype(BF16),
        conv_rnn_w=conv_rnn_w, conv_rnn_b=row(conv_rnn_b),
        wa=_block_diag4(rg_w_a), ba=row(rg_b_a), wx=_block_diag4(rg_w_x), bx=row(rg_b_x), lam=row(rg_lambda),
        conv_ssm_w=conv_ssm_w, conv_ssm_b=row(conv_ssm_b),
        dt_bias=pad_l(dt_bias), a_log=pad_l(a_log),
        d_skip=row(jnp.repeat(d_skip, SSM_HEAD_DIM)), ssm_norm=row(ssm_norm),
        w_out0a=w_out0[:D_RNN].astype(BF16), w_out0b=w_out0[D_RNN:].astype(BF16),
        w_in1=w_in1.astype(BF16),
        lam4=jnp.stack([lambda_q1, lambda_k1, lambda_q2, lambda_k2]), subln=row(subln),
        w_out1=w_out1.astype(BF16),
        w_up=[w_up[l].astype(BF16) for l in range(2)],
        ffn_conv_w=[ffn_conv_w[l] for l in range(2)], ffn_conv_b=[row(ffn_conv_b[l]) for l in range(2)],
        w_down=[w_down[l].astype(BF16) for l in range(2)],
    )


def _state_t(s):
    b = s.shape[0]
    return jnp.transpose(s, (0, 3, 1, 2)).reshape(b, SSM_STATE, D_SSM)


def _state_untranspose(st):
    b = st.shape[0]
    return jnp.transpose(st.reshape(b, SSM_STATE, SSM_HEADS, SSM_HEAD_DIM), (0, 2, 3, 1))


def _prompt_trunk(x_prompt, w):
    bsz, seq, d = x_prompt.shape
    m = bsz * seq
    x = x_prompt.reshape(m, d)
    tm = min(ROW_TILE, seq)
    tr = min(RNN_TILE, seq)

    p0 = proj0(x, w['norm_mix'][0], w['w_in0'], tm)
    rnn_out, h_tail = rglru(p0, bsz, seq, tr, w['conv_rnn_w'], w['conv_rnn_b'], w['wa'], w['ba'], w['wx'],
                            w['bx'], w['lam'])
    lc = SSD_CHUNK
    nch = seq // lc
    hb = lc // HALO
    xbc_cb = (2 * D_RNN + D_SSM) // SSM_CONV_DIM
    halo_spec = pl.BlockSpec(
        (HALO, SSM_CONV_DIM), lambda b, i: (jnp.maximum((b * nch + i) * hb - 1, 0), xbc_cb))
    y_ssm, st = ssd(p0, xbc_cb, p0, halo_spec, p0, 2 * D_RNN // D_SSM, p0, (D_IN0_PAD - LANES) // LANES,
                    bsz, nch, lc, lc, w['conv_ssm_w'], w['conv_ssm_b'], w['dt_bias'], w['a_log'],
                    w['d_skip'], w['ssm_norm'])
    x = out_proj([rnn_out, y_ssm], [w['w_out0a'], w['w_out0b']], x, tm)
    x, u0 = ffn(x, bsz, seq, tm, w['norm_ffn'][0], w['w_up'][0], w['ffn_conv_w'][0], w['ffn_conv_b'][0],
                w['w_down'][0], w['norm_final'], False)

    k, v, qkv = proj1(x, w['norm_mix'][1], w['w_in1'], tm)
    o = attn_prompt(qkv, bsz, seq, min(ATT_Q_TILE, seq), w['lam4'], w['subln'], tk=min(ATT_K_TILE, seq))
    x = out_proj([o], [w['w_out1']], x, tm)
    y, u1 = ffn(x, bsz, seq, tm, w['norm_ffn'][1], w['w_up'][1], w['ffn_conv_w'][1], w['ffn_conv_b'][1],
                w['w_down'][1], w['norm_final'], True)

    p3 = p0.reshape(bsz, seq, D_IN0_PAD)
    rnn_conv = p3[:, seq - (CONV_W - 1):, 0:D_RNN]
    ssm_conv = p3[:, seq - (CONV_W - 1):, 2 * D_RNN + D_SSM:2 * D_RNN + D_SSM + SSM_CONV_DIM]
    ffn_conv = jnp.stack([u0[:, SUBLANES - (FFN_CONV_W - 1):], u1[:, SUBLANES - (FFN_CONV_W - 1):]])
    return (y.reshape(bsz, seq, d), rnn_conv, h_tail[:, SUBLANES - 1], ssm_conv, _state_untranspose(st),
            k.reshape(bsz, seq, ATT_HEADS, 2 * ATT_HEAD_DIM), v.reshape(bsz, seq, ATT_HEADS, ATT_V_DIM), ffn_conv)


def _to_groups(x_tok, hist=None):
    b, t, c = x_tok.shape
    if hist is None:
        lead = jnp.zeros((b, TOK0, c), x_tok.dtype)
    else:
        k = hist.shape[1]
        lead = jnp.concatenate([jnp.zeros((b, TOK0 - k, c), x_tok.dtype), hist], axis=1)
    return jnp.concatenate([lead, x_tok], axis=1).reshape(b * GROUP, c)


def _sample_trunk(x_sample, state_rnn_conv, state_rnn_h, state_ssm_conv, state_ssm, cache_k, cache_v,
                  state_ffn_conv, page_table, w):
    nb, t, d = x_sample.shape
    assert t == GROUP - TOK0
    m = nb * GROUP
    x = _to_groups(x_sample)

    p0 = proj0(x, w['norm_mix'][0], w['w_in0'], m)
    p0g = p0.reshape(nb, GROUP, D_IN0_PAD)
    c_xbc = 2 * D_RNN + D_SSM
    xr = _to_groups(p0g[:, TOK0:, 0:D_RNN], state_rnn_conv)
    p0r = jnp.concatenate([xr, p0[:, D_RNN:2 * D_RNN]], axis=1)
    hinj = _to_groups(jnp.zeros((nb, t, D_RNN), F32), state_rnn_h[:, None, :])
    rnn_out, h_all = rglru(p0r, 1, m, m, w['conv_rnn_w'], w['conv_rnn_b'], w['wa'], w['ba'], w['wx'], w['bx'],
                           w['lam'], hinj=hinj)
    lc = SSD_CHUNK
    padc = lambda a: jnp.pad(a, ((0, 0), (0, lc - t), (0, 0))).reshape(nb * lc, a.shape[-1])
    xbc_p = padc(p0g[:, TOK0:, c_xbc:c_xbc + SSM_CONV_DIM])
    z_p = padc(p0g[:, TOK0:, 2 * D_RNN:2 * D_RNN + D_SSM])
    dt_p = padc(p0g[:, TOK0:, D_IN0_PAD - LANES:])
    halo = jnp.concatenate([jnp.zeros((nb, HALO - (CONV_W - 1), SSM_CONV_DIM), F32), state_ssm_conv], axis=1)
    halo_spec = pl.BlockSpec((None, HALO, SSM_CONV_DIM), lambda b, i: (b, 0, 0))
    y_p, st = ssd(xbc_p, 0, halo, halo_spec, z_p, 0, dt_p, 0, nb, 1, lc, t, w['conv_ssm_w'], w['conv_ssm_b'],
                  w['dt_bias'], w['a_log'], w['d_skip'], w['ssm_norm'], s0=_state_t(state_ssm))
    y_ssm = _to_groups(y_p.reshape(nb, lc, D_SSM)[:, 0:t])
    x = out_proj([rnn_out, y_ssm], [w['w_out0a'], w['w_out0b']], x, m)
    uinj0 = _to_groups(jnp.zeros((nb, t, 2 * D_FF), F32), state_ffn_conv[0])
    x, u0 = ffn(x, 1, m, m, w['norm_ffn'][0], w['w_up'][0], w['ffn_conv_w'][0], w['ffn_conv_b'][0],
                w['w_down'][0], w['norm_final'], False, uinj=uinj0)

    k, v, qkv = proj1(x, w['norm_mix'][1], w['w_in1'], m)
    qg = qkv[:, 0:D_MODEL].reshape(nb, GROUP, ATT_HEADS, 2, ATT_HEAD_DIM)[:, TOK0:]
    flip = jnp.array([[0, 1], [1, 0]], dtype=BF16)
    qh = jnp.einsum('bqhmd,nm->bhnqmd', qg, flip).reshape(nb, ATT_HEADS * GROUP, 2 * ATT_HEAD_DIM)
    o = attn_sample(page_table, cache_k, cache_v, qh, k, v, w['lam4'], w['subln'],
                    npp=min(PAGES_PER_STEP, page_table.shape[1]))
    x = out_proj([o], [w['w_out1']], x, m)
    uinj1 = _to_groups(jnp.zeros((nb, t, 2 * D_FF), F32), state_ffn_conv[1])
    y, u1 = ffn(x, 1, m, m, w['norm_ffn'][1], w['w_up'][1], w['ffn_conv_w'][1], w['ffn_conv_b'][1],
                w['w_down'][1], w['norm_final'], True, uinj=uinj1)

    tok = lambda a: a.reshape(nb, GROUP, a.shape[-1])[:, TOK0:]
    rnn_conv = p0g[:, GROUP - (CONV_W - 1):, 0:D_RNN]
    ssm_conv = p0g[:, GROUP - (CONV_W - 1):, c_xbc:c_xbc + SSM_CONV_DIM]
    ffn_conv = jnp.stack([u0.reshape(nb, GROUP, -1)[:, GROUP - (FFN_CONV_W - 1):],
                          u1.reshape(nb, GROUP, -1)[:, GROUP - (FFN_CONV_W - 1):]])
    return (tok(y), rnn_conv, h_all.reshape(nb, GROUP, D_RNN)[:, GROUP - 1], ssm_conv, _state_untranspose(st),
            tok(k).reshape(nb, t, ATT_HEADS, 2 * ATT_HEAD_DIM), tok(v).reshape(nb, t, ATT_HEADS, ATT_V_DIM),
            ffn_conv)


def kernel(x_prompt, x_sample, state_rnn_conv, state_rnn_h, state_ssm_conv, state_ssm, cache_k, cache_v, state_ffn_conv, page_table, norm_mix, norm_ffn, norm_final, w_in0, conv_rnn_w, conv_rnn_b, rg_w_a, rg_b_a, rg_w_x, rg_b_x, rg_lambda, conv_ssm_w, conv_ssm_b, dt_bias, a_log, d_skip, ssm_norm, w_out0, w_in1, lambda_q1, lambda_k1, lambda_q2, lambda_k2, subln, w_out1, w_up, ffn_conv_w, ffn_conv_b, w_down):
    w = _prep_weights(norm_mix, norm_ffn, norm_final, w_in0, conv_rnn_w, conv_rnn_b, rg_w_a, rg_b_a, rg_w_x,
                      rg_b_x, rg_lambda, conv_ssm_w, conv_ssm_b, dt_bias, a_log, d_skip, ssm_norm, w_out0, w_in1,
                      lambda_q1, lambda_k1, lambda_q2, lambda_k2, subln, w_out1, w_up, ffn_conv_w, ffn_conv_b,
                      w_down)
    p = _prompt_trunk(x_prompt, w)
    s = _sample_trunk(x_sample, state_rnn_conv, state_rnn_h, state_ssm_conv, state_ssm, cache_k, cache_v,
                      state_ffn_conv, page_table, w)
    return (p[0], s[0]) + p[1:] + s[1:]
```
